```python
import math
import jax
import jax.numpy as jnp
from jax import lax
import numpy as np

D_MODEL = 1024
BATCH = 2
SEQ = 8192
DEPTH = 2
DEC_BATCH = 128
DEC_SEQ = 8
PAST_LEN = 16384
PAGE_SIZE = 128

HD = 64
N_BIAS_HEADS = 8
NSA_H = N_BIAS_HEADS
NSA_G = 2
NSA_HPG = NSA_H // NSA_G
CMP_STRIDE = 16
CMP_LEN = 2 * CMP_STRIDE
CMP_HID = 128
SEL_BLK = 64
SEL_TOPK = 16
NSA_WINDOW = 512
MLA_H = 4
Q_LORA = 256
KV_LORA = 128
NOPE_D = 64
ROPE_D = 32
QK_D = NOPE_D + ROPE_D
V_D = 64
ROPE_THETA = 10000.0
MOBA_H = N_BIAS_HEADS
MOBA_KVH = 2
MOBA_HPG = MOBA_H // MOBA_KVH
MOBA_BLK = 256
MOBA_TOPK = 3
N_MEM = 256
MEM_H = 4
N_BUCKETS = 32
T5_MAX_DIST = 2048
Q_BLOCK = 128
N_EVEN = (DEPTH + 1) // 2
N_ODD = DEPTH // 2
EVEN_SIZES = (NSA_H * HD, 6 * NSA_G * HD, 3 * NSA_H, NSA_H * HD, Q_LORA, KV_LORA, ROPE_D, MLA_H * V_D, MEM_H * HD, MEM_H * HD)
ODD_SIZES = (MOBA_H * HD, 2 * MOBA_KVH * HD, MOBA_H * HD, MEM_H * HD, MEM_H * HD)
EVEN_OUT = NSA_H * HD + MLA_H * V_D + MEM_H * HD
ODD_OUT = MOBA_H * HD + MEM_H * HD
EPS = 1e-6
NEG = -1e30
FORCE = 1e9

kernel_name = 'hybrid_nsa_mla_moba_mem_step'


def rms_norm(x, g):
    xf = x.astype(jnp.float32)
    y = xf * lax.rsqrt(jnp.mean(xf * xf, axis=-1, keepdims=True) + EPS)
    return (y * g.astype(jnp.float32)).astype(x.dtype)


def split_cols(h, sizes):
    return jnp.split(h, np.cumsum(sizes)[:-1].tolist(), axis=-1)


def masked_softmax(s, mask):
    p = jax.nn.softmax(jnp.where(mask, s, NEG), axis=-1)
    return p * mask


def t5_bucket(dist):
    dist = jnp.maximum(dist, 0)
    exact = N_BUCKETS // 2
    far = exact + (jnp.log(jnp.maximum(dist, 1).astype(jnp.float32) / exact)
                   / math.log(T5_MAX_DIST / exact) * (N_BUCKETS - exact)).astype(jnp.int32)
    return jnp.where(dist < exact, dist, jnp.minimum(far, N_BUCKETS - 1))


def rope(x, pos):
    half = ROPE_D // 2
    inv = ROPE_THETA ** (-jnp.arange(half, dtype=jnp.float32) / half)
    ang = pos.astype(jnp.float32)[:, None] * inv
    ang = ang.reshape((ang.shape[0],) + (1,) * (x.ndim - 3) + (half,))
    cos, sin = jnp.cos(ang), jnp.sin(ang)
    xf = x.astype(jnp.float32)
    x1, x2 = xf[..., :half], xf[..., half:]
    return jnp.concatenate([x1 * cos - x2 * sin, x2 * cos + x1 * sin], axis=-1).astype(x.dtype)


def map_query_blocks(fn, n_q, qb):
    starts = jnp.arange(n_q // qb, dtype=jnp.int32) * qb
    out = jnp.moveaxis(lax.map(fn, starts), 0, 1)
    return out.reshape((out.shape[0], n_q) + out.shape[3:])


def contig_rows(arr):
    n = arr.shape[1]
    def get(pos, hidx):
        b = jnp.arange(arr.shape[0]).reshape((-1,) + (1,) * (pos.ndim - 1))
        return arr[b, jnp.clip(pos, 0, n - 1), hidx]
    return get


def paged_rows(pool, li, page_table, new):
    past = page_table.shape[1] * PAGE_SIZE
    s_new = new.shape[1]
    def get(pos, hidx):
        b = jnp.arange(new.shape[0]).reshape((-1,) + (1,) * (pos.ndim - 1))
        pp = jnp.clip(pos, 0, past - 1)
        old = pool[li, page_table[b, pp // PAGE_SIZE], pp % PAGE_SIZE, hidx]
        cur = new[b, jnp.clip(pos - past, 0, s_new - 1), hidx]
        return jnp.where((pos < past)[..., None], old, cur)
    return get


def gather_past(pool, li, page_table):
    b, n_pages = page_table.shape
    return pool[li, page_table].reshape((b, n_pages * PAGE_SIZE) + pool.shape[3:])


def cmp_chunk_proj(raw, w1):
    B, L = raw.shape[:2]
    ch = raw.reshape(B, L // CMP_STRIDE, CMP_STRIDE, NSA_G, HD)
    a = jnp.einsum('bnsgd,sdh->bngh', ch, w1[:CMP_STRIDE])
    b = jnp.einsum('bnsgd,sdh->bngh', ch, w1[CMP_STRIDE:])
    return a, b


def cmp_finish(a, b, pe, w1, w2):
    pe_h = jnp.einsum('sd,sdh->h', pe, w1)
    h = jax.nn.gelu(a[:, :-1] + b[:, 1:] + pe_h)
    return jnp.einsum('bngh,hd->bngd', h, w2)


def nsa_compress(raw_k_parts, raw_v_parts, W):
    out = []
    for j, parts in enumerate((raw_k_parts, raw_v_parts)):
        w1 = W['cmp_w1'][j]
        ab = [cmp_chunk_proj(r, w1) for r in parts]
        a = jnp.concatenate([t[0] for t in ab], axis=1)
        b = jnp.concatenate([t[1] for t in ab], axis=1)
        out.append(cmp_finish(a, b, W['cmp_pe'][j], w1, W['cmp_w2'][j]))
    return rms_norm(out[0], W['nsa_kn'][0]), out[1]


def sel_block_scores(imp, nsb):
    r = SEL_BLK // CMP_STRIDE
    front = CMP_LEN // CMP_STRIDE - 1
    nc = imp.shape[-1]
    imp = jnp.pad(imp, [(0, 0)] * (imp.ndim - 1) + [(front, r * nsb - nc)])
    score = 0.0
    for o in range(-front, r):
        lo = o * CMP_STRIDE
        w = max(0, min(lo + CMP_LEN, SEL_BLK) - max(lo, 0)) / CMP_LEN
        start = o + front
        score = score + w * imp[..., start:start + r * (nsb - 1) + 1:r]
    return score


def nsa_block(q, gates, q_pos, kc, vc, get_k, get_v, nsb, kw, vw, kw_pos, bias_t):
    B, Q = q.shape[:2]
    scale = HD ** -0.5
    f32 = jnp.float32
    qg = q.reshape(B, Q, NSA_G, NSA_HPG, HD)
    tg = bias_t.T.reshape(NSA_G, NSA_HPG, N_BUCKETS)
    nc = kc.shape[1]
    dist_c = q_pos[:, None] - (jnp.arange(nc) * CMP_STRIDE + CMP_LEN - 1)[None, :]
    bias_c = jnp.transpose(tg[:, :, t5_bucket(dist_c)], (2, 0, 1, 3))[None]
    s_c = jnp.einsum('bqgpd,bngd->bqgpn', qg, kc, preferred_element_type=f32) * scale + bias_c
    p_c = masked_softmax(s_c, (dist_c >= 0)[None, :, None, None, :])
    o_c = jnp.einsum('bqgpn,bngd->bqgpd', p_c.astype(vc.dtype), vc)
    score = sel_block_scores(p_c.sum(axis=3), nsb)
    blk = jnp.arange(nsb)[None, :]
    own = (q_pos // SEL_BLK)[:, None]
    forced = (blk == 0) | (blk == own) | (blk == own - 1)
    score = jnp.where(forced[None, :, None, :], FORCE,
                      jnp.where((blk <= own)[None, :, None, :], score, -jnp.inf))
    k_eff = min(SEL_TOPK, nsb)
    _, idx = lax.top_k(score, k_eff)
    pos_s = (idx[..., None] * SEL_BLK + jnp.arange(SEL_BLK)).reshape(B, Q, NSA_G, k_eff * SEL_BLK)
    gidx = jnp.arange(NSA_G)[None, None, :, None]
    k_s, v_s = get_k(pos_s, gidx), get_v(pos_s, gidx)
    dist_s = q_pos[None, :, None, None] - pos_s
    bias_s = tg[gidx[..., None], jnp.arange(NSA_HPG)[None, None, None, :, None], t5_bucket(dist_s)[:, :, :, None, :]]
    s_s = jnp.einsum('bqgpd,bqgrd->bqgpr', qg, k_s, preferred_element_type=f32) * scale + bias_s
    p_s = masked_softmax(s_s, (dist_s >= 0)[:, :, :, None, :])
    o_s = jnp.einsum('bqgpr,bqgrd->bqgpd', p_s.astype(v_s.dtype), v_s)
    dist_w = q_pos[:, None] - kw_pos[None, :]
    mask_w = (dist_w >= 0) & (dist_w < NSA_WINDOW) & (kw_pos >= 0)[None, :]
    bias_w = jnp.transpose(tg[:, :, t5_bucket(dist_w)], (2, 0, 1, 3))[None]
    s_w = jnp.einsum('bqgpd,blgd->bqgpl', qg, kw, preferred_element_type=f32) * scale + bias_w
    p_w = masked_softmax(s_w, mask_w[None, :, None, None, :])
    o_w = jnp.einsum('bqgpl,blgd->bqgpd', p_w.astype(vw.dtype), vw)
    g = gates.reshape(B, Q, NSA_G, NSA_HPG, 3, 1)
    o = g[..., 0, :] * o_c + g[..., 1, :] * o_s + g[..., 2, :] * o_w
    return o.reshape(B, Q, NSA_H * HD)


def mla_keys(ckv, kr, k_pos, W):
    kn = jnp.einsum('blc,chd->blhd', ckv, W['mla_wuk'])
    krf = kr.astype(jnp.float32)
    ss = jnp.einsum('blhd,blhd->blh', kn, kn, preferred_element_type=jnp.float32) + jnp.sum(krf * krf, -1)[..., None]
    inv = lax.rsqrt(ss / QK_D + EPS)
    krr = rope(kr * W['mla_kn'][NOPE_D:], k_pos)
    return kn, inv, krr


def mla_block(q_nope, q_rope, q_pos, keys, ckv, k_pos, W):
    kn, inv, krr = keys
    B, Q = q_nope.shape[:2]
    s = (jnp.einsum('bqhd,blhd->bhql', q_nope * W['mla_kn'][:NOPE_D], kn, preferred_element_type=jnp.float32)
         + jnp.einsum('bqhd,bld->bhql', q_rope, krr, preferred_element_type=jnp.float32))
    s = s * jnp.transpose(inv, (0, 2, 1))[:, :, None, :] * QK_D ** -0.5
    p = masked_softmax(s, (k_pos[None, :] <= q_pos[:, None])[None, None])
    o_lat = jnp.einsum('bhql,blc->bqhc', p.astype(ckv.dtype), ckv)
    return jnp.einsum('bqhc,chd->bqhd', o_lat, W['mla_wuv']).reshape(B, Q, MLA_H * V_D)


def moba_means_paged(pool_k, li, page_table, k_new):
    B, n_pages = page_table.shape
    past = n_pages * PAGE_SIZE
    total = past + k_new.shape[1]
    nbm = -(-total // MOBA_BLK)
    pps = MOBA_BLK // PAGE_SIZE
    page_sum = pool_k[li, page_table].astype(jnp.float32).sum(axis=2)
    npp = -(-n_pages // pps) * pps
    page_sum = jnp.pad(page_sum, ((0, 0), (0, npp - n_pages), (0, 0), (0, 0)))
    blk_sum = page_sum.reshape(B, npp // pps, pps, MOBA_KVH, HD).sum(axis=2)
    blk_sum = jnp.pad(blk_sum, ((0, 0), (0, nbm - npp // pps), (0, 0), (0, 0)))
    new_blk = (past + jnp.arange(k_new.shape[1])) // MOBA_BLK
    onehot = (new_blk[:, None] == jnp.arange(nbm)[None, :]).astype(jnp.float32)
    blk_sum = blk_sum + jnp.einsum('bsgd,sn->bngd', k_new.astype(jnp.float32), onehot)
    return blk_sum / MOBA_BLK


def moba_block(q, q_pos, kmean_h, get_k, get_v, bias_t):
    B, Q = q.shape[:2]
    nbm = kmean_h.shape[1]
    own = q_pos // MOBA_BLK
    own_pos = own[:, None] * MOBA_BLK + jnp.arange(MOBA_BLK)
    shp = (B, Q, MOBA_H, MOBA_BLK)
    pos = jnp.broadcast_to(own_pos[None, :, None, :], shp)
    ok = jnp.ones(shp, bool)
    k_eff = min(MOBA_TOPK, nbm - 1)
    if k_eff > 0:
        gs = jnp.einsum('bqhd,bnhd->bqhn', q, kmean_h, preferred_element_type=jnp.float32)
        full_past = jnp.arange(nbm)[None, :] < own[:, None]
        gs = jnp.where(full_past[None, :, None, :], gs, -jnp.inf)
        _, idx = lax.top_k(gs, k_eff)
        pos_sel = (idx[..., None] * MOBA_BLK + jnp.arange(MOBA_BLK)).reshape(B, Q, MOBA_H, k_eff * MOBA_BLK)
        ok_sel = jnp.repeat(idx < own[None, :, None, None], MOBA_BLK, axis=-1)
        pos = jnp.concatenate([pos_sel, pos], axis=-1)
        ok = jnp.concatenate([ok_sel, ok], axis=-1)
    hidx = (jnp.arange(MOBA_H) // MOBA_HPG)[None, None, :, None]
    k, v = get_k(pos, hidx), get_v(pos, hidx)
    dist = q_pos[None, :, None, None] - pos
    bias = bias_t.T[jnp.arange(MOBA_H)[None, None, :, None], t5_bucket(dist)]
    s = jnp.einsum('bqhd,bqhrd->bqhr', q, k, preferred_element_type=jnp.float32) * HD ** -0.5 + bias
    p = masked_softmax(s, ok & (dist >= 0))
    o = jnp.einsum('bqhr,bqhrd->bqhd', p.astype(v.dtype), v)
    return o.reshape(B, Q, MOBA_H * HD)


def mem_kv(mem, g, wkv, kn):
    B, N = mem.shape[:2]
    h = (rms_norm(mem, g) @ wkv).reshape(B, N, 2, MEM_H, HD)
    return rms_norm(h[:, :, 0], kn), h[:, :, 1]


def mem_attend(q, km, vm):
    B, Q = q.shape[:2]
    s = jnp.einsum('bqhd,bmhd->bhqm', q, km, preferred_element_type=jnp.float32) * HD ** -0.5
    p = jax.nn.softmax(s, axis=-1).astype(vm.dtype)
    return jnp.einsum('bhqm,bmhd->bqhd', p, vm).reshape(B, Q, MEM_H * HD)


def even_project(x, ln, W, pos):
    B, T = x.shape[:2]
    h = rms_norm(x, ln) @ W['w_in']
    nq, nkv, ngt, nz, cq, ckv, kr, mz, mq, memz = split_cols(h, EVEN_SIZES)
    kv = nkv.reshape(B, T, 6, NSA_G, HD)
    qm = (rms_norm(cq, W['mla_cqn']) @ W['mla_wuq']).reshape(B, T, MLA_H, QK_D)
    qm = rms_norm(qm, W['mla_qn'])
    return {
        'q': rms_norm(nq.reshape(B, T, NSA_H, HD), W['nsa_qn']),
        'gates': jax.nn.sigmoid(ngt.reshape(B, T, NSA_H, 3)),
        'raw_kc': kv[:, :, 0], 'raw_vc': kv[:, :, 1],
        'k_sel': rms_norm(kv[:, :, 2], W['nsa_kn'][1]), 'v_sel': kv[:, :, 3],
        'k_win': rms_norm(kv[:, :, 4], W['nsa_kn'][2]), 'v_win': kv[:, :, 5],
        'q_nope': qm[..., :NOPE_D], 'q_rope': rope(qm[..., NOPE_D:], pos),
        'ckv': rms_norm(ckv, W['mla_ckvn']), 'kr': kr,
        'qmem': rms_norm(mq.reshape(B, T, MEM_H, HD), W['mem_qn']),
        'z': (nz, mz, memz),
    }


def even_finish(P, o_nsa, o_mla, o_mem, W):
    z_nsa, z_mla, z_mem = P['z']
    mixed = jnp.concatenate([o_nsa * jax.nn.silu(z_nsa), o_mla * jax.nn.silu(z_mla), o_mem * jax.nn.silu(z_mem)], axis=-1)
    return mixed @ W['w_out']


def even_prompt(x, ln, W, km, vm, bias_t):
    B, T = x.shape[:2]
    pos = jnp.arange(T, dtype=jnp.int32)
    P = even_project(x, ln, W, pos)
    kc, vc = nsa_compress([P['raw_kc']], [P['raw_vc']], W)
    nsb = T // SEL_BLK
    get_k, get_v = contig_rows(P['k_sel']), contig_rows(P['v_sel'])
    padw = ((0, 0), (NSA_WINDOW, 0), (0, 0), (0, 0))
    kwp, vwp = jnp.pad(P['k_win'], padw), jnp.pad(P['v_win'], padw)
    qb = min(Q_BLOCK, T)

    def nsa_fn(c):
        sl = lambda a, n=qb: lax.dynamic_slice_in_dim(a, c, n, 1)
        return nsa_block(sl(P['q']), sl(P['gates']), c + jnp.arange(qb), kc, vc, get_k, get_v, nsb,
                         sl(kwp, NSA_WINDOW + qb), sl(vwp, NSA_WINDOW + qb),
                         c - NSA_WINDOW + jnp.arange(NSA_WINDOW + qb), bias_t)
    o_nsa = map_query_blocks(nsa_fn, T, qb)
    keys = mla_keys(P['ckv'], P['kr'], pos, W)

    def mla_fn(c):
        sl = lambda a: lax.dynamic_slice_in_dim(a, c, qb, 1)
        return mla_block(sl(P['q_nope']), sl(P['q_rope']), c + jnp.arange(qb), keys, P['ckv'], pos, W)
    o_mla = map_query_blocks(mla_fn, T, qb)
    o_mem = mem_attend(P['qmem'], km, vm)
    y = even_finish(P, o_nsa, o_mla, o_mem, W)
    wk = min(NSA_WINDOW, T)
    state = (P['raw_kc'], P['raw_vc'], P['k_sel'], P['v_sel'], P['k_win'][:, T - wk:], P['v_win'][:, T - wk:], P['ckv'], P['kr'])
    return y, state


def even_sample(x, ln, W, km, vm, bias_t, caches, li, page_table):
    c_cmp_k, c_cmp_v, c_sel_k, c_sel_v, s_win_k, s_win_v, c_ckv, c_kr = caches
    B, S = x.shape[:2]
    past = page_table.shape[1] * PAGE_SIZE
    total = past + S
    pos = past + jnp.arange(S, dtype=jnp.int32)
    P = even_project(x, ln, W, pos)
    l_pad = -(-total // SEL_BLK) * SEL_BLK
    padn = ((0, 0), (0, l_pad - total), (0, 0), (0, 0))
    kc, vc = nsa_compress([gather_past(c_cmp_k, li, page_table), jnp.pad(P['raw_kc'], padn)],
                          [gather_past(c_cmp_v, li, page_table), jnp.pad(P['raw_vc'], padn)], W)
    nsb = l_pad // SEL_BLK
    get_k = paged_rows(c_sel_k, li, page_table, P['k_sel'])
    get_v = paged_rows(c_sel_v, li, page_table, P['v_sel'])
    kw = jnp.concatenate([s_win_k[li], P['k_win']], axis=1)
    vw = jnp.concatenate([s_win_v[li], P['v_win']], axis=1)
    wb = s_win_k.shape[2]
    kw_pos = past - wb + jnp.arange(wb + S)

    def nsa_fn(c):
        sl = lambda a: lax.dynamic_slice_in_dim(a, c, 1, 1)
        return nsa_block(sl(P['q']), sl(P['gates']), past + c + jnp.arange(1), kc, vc, get_k, get_v, nsb,
                         kw, vw, kw_pos, bias_t)
    o_nsa = map_query_blocks(nsa_fn, S, 1)
    ckv_all = jnp.concatenate([gather_past(c_ckv, li, page_table), P['ckv']], axis=1)
    kr_all = jnp.concatenate([gather_past(c_kr, li, page_table), P['kr']], axis=1)
    k_pos = jnp.arange(total, dtype=jnp.int32)
    o_mla = mla_block(P['q_nope'], P['q_rope'], pos, mla_keys(ckv_all, kr_all, k_pos, W), ckv_all, k_pos, W)
    o_mem = mem_attend(P['qmem'], km, vm)
    y = even_finish(P, o_nsa, o_mla, o_mem, W)
    wk = min(NSA_WINDOW, total)
    n_w = kw.shape[1]
    state = (P['raw_kc'], P['raw_vc'], P['k_sel'], P['v_sel'], kw[:, n_w - wk:], vw[:, n_w - wk:], P['ckv'], P['kr'])
    return y, state


def odd_project(x, ln, W):
    B, T = x.shape[:2]
    h = rms_norm(x, ln) @ W['w_in']
    mq, mkv, mz, memq, memz = split_cols(h, ODD_SIZES)
    kv = mkv.reshape(B, T, 2, MOBA_KVH, HD)
    return {
        'q': rms_norm(mq.reshape(B, T, MOBA_H, HD), W['moba_qn']),
        'k': rms_norm(kv[:, :, 0], W['moba_kn']), 'v': kv[:, :, 1],
        'qmem': rms_norm(memq.reshape(B, T, MEM_H, HD), W['mem_qn']),
        'z': (mz, memz),
    }


def odd_finish(P, o_moba, o_mem, W):
    z_moba, z_mem = P['z']
    mixed = jnp.concatenate([o_moba * jax.nn.silu(z_moba), o_mem * jax.nn.silu(z_mem)], axis=-1)
    return mixed @ W['w_out']


def odd_prompt(x, ln, W, km, vm, bias_t):
    B, T = x.shape[:2]
    P = odd_project(x, ln, W)
    nbm = -(-T // MOBA_BLK)
    kf = jnp.pad(P['k'].astype(jnp.float32), ((0, 0), (0, nbm * MOBA_BLK - T), (0, 0), (0, 0)))
    kmean = kf.reshape(B, nbm, MOBA_BLK, MOBA_KVH, HD).sum(axis=2) / MOBA_BLK
    kmean_h = jnp.repeat(kmean, MOBA_HPG, axis=2)
    get_k, get_v = contig_rows(P['k']), contig_rows(P['v'])
    qb = min(Q_BLOCK, T)

    def moba_fn(c):
        return moba_block(lax.dynamic_slice_in_dim(P['q'], c, qb, 1), c + jnp.arange(qb), kmean_h, get_k, get_v, bias_t)
    o_moba = map_query_blocks(moba_fn, T, qb)
    y = odd_finish(P, o_moba, mem_attend(P['qmem'], km, vm), W)
    return y, (P['k'], P['v'])


def odd_sample(x, ln, W, km, vm, bias_t, c_k, c_v, li, page_table):
    B, S = x.shape[:2]
    past = page_table.shape[1] * PAGE_SIZE
    P = odd_project(x, ln, W)
    kmean_h = jnp.repeat(moba_means_paged(c_k, li, page_table, P['k']), MOBA_HPG, axis=2)
    get_k = paged_rows(c_k, li, page_table, P['k'])
    get_v = paged_rows(c_v, li, page_table, P['v'])

    def moba_fn(c):
        return moba_block(lax.dynamic_slice_in_dim(P['q'], c, 1, 1), past + c + jnp.arange(1), kmean_h, get_k, get_v, bias_t)
    o_moba = map_query_blocks(moba_fn, S, 1)
    y = odd_finish(P, o_moba, mem_attend(P['qmem'], km, vm), W)
    return y, (P['k'], P['v'])


def setup_inputs(seed: int = 0) -> dict:
    key = jax.random.key(seed)
    keys = iter(jax.random.split(key, 64))

    def nrm(shape, scale=1.0):
        return jax.random.normal(next(keys), shape, jnp.float32) * scale

    def gain(shape):
        return 1.0 + 0.05 * jax.random.normal(next(keys), shape, jnp.float32)

    n_pages = PAST_LEN // PAGE_SIZE
    n_used = DEC_BATCH * n_pages
    n_pool = n_used + n_used // 4
    wb = min(NSA_WINDOW, PAST_LEN)
    page_table = jax.random.permutation(next(keys), n_pool)[:n_used].reshape(DEC_BATCH, n_pages).astype(jnp.int32)
    return {
        'x_prompt': nrm((BATCH, SEQ, D_MODEL)),
        'x_sample': nrm((DEC_BATCH, DEC_SEQ, D_MODEL)),
        'cache_nsa_cmp_k': nrm((N_EVEN, n_pool, PAGE_SIZE, NSA_G, HD)),
        'cache_nsa_cmp_v': nrm((N_EVEN, n_pool, PAGE_SIZE, NSA_G, HD)),
        'cache_nsa_sel_k': nrm((N_EVEN, n_pool, PAGE_SIZE, NSA_G, HD)),
        'cache_nsa_sel_v': nrm((N_EVEN, n_pool, PAGE_SIZE, NSA_G, HD)),
        'state_nsa_win_k': nrm((N_EVEN, DEC_BATCH, wb, NSA_G, HD)),
        'state_nsa_win_v': nrm((N_EVEN, DEC_BATCH, wb, NSA_G, HD)),
        'cache_mla_ckv': nrm((N_EVEN, n_pool, PAGE_SIZE, KV_LORA)),
        'cache_mla_krope': nrm((N_EVEN, n_pool, PAGE_SIZE, ROPE_D)),
        'cache_moba_k': nrm((N_ODD, n_pool, PAGE_SIZE, MOBA_KVH, HD)),
        'cache_moba_v': nrm((N_ODD, n_pool, PAGE_SIZE, MOBA_KVH, HD)),
        'cache_mem_k': nrm((DEPTH, DEC_BATCH, N_MEM, MEM_H, HD)),
        'cache_mem_v': nrm((DEPTH, DEC_BATCH, N_MEM, MEM_H, HD)),
        'page_table': page_table,
        'mem_prompt': nrm((BATCH, N_MEM, D_MODEL)),
        'rel_bias': nrm((N_BUCKETS, N_BIAS_HEADS), 0.2),
        'ln_g': gain((DEPTH, D_MODEL)),
        'mem_norm_g': gain((DEPTH, D_MODEL)),
        'mem_wkv': nrm((DEPTH, D_MODEL, 2 * MEM_H * HD), D_MODEL ** -0.5),
        'mem_qn': gain((DEPTH, HD)),
        'mem_kn': gain((DEPTH, HD)),
        'e_w_in': nrm((N_EVEN, D_MODEL, sum(EVEN_SIZES)), D_MODEL ** -0.5),
        'e_w_out': nrm((N_EVEN, EVEN_OUT, D_MODEL), EVEN_OUT ** -0.5),
        'nsa_qn': gain((N_EVEN, HD)),
        'nsa_kn': gain((N_EVEN, 3, HD)),
        'nsa_cmp_pe': nrm((N_EVEN, 2, CMP_LEN, HD), 0.5),
        'nsa_cmp_w1': nrm((N_EVEN, 2, CMP_LEN, HD, CMP_HID), (CMP_LEN * HD) ** -0.5),
        'nsa_cmp_w2': nrm((N_EVEN, 2, CMP_HID, HD), CMP_HID ** -0.5),
        'mla_cqn': gain((N_EVEN, Q_LORA)),
        'mla_wuq': nrm((N_EVEN, Q_LORA, MLA_H * QK_D), Q_LORA ** -0.5),
        'mla_ckvn': gain((N_EVEN, KV_LORA)),
        'mla_wuk': nrm((N_EVEN, KV_LORA, MLA_H, NOPE_D), KV_LORA ** -0.5),
        'mla_wuv': nrm((N_EVEN, KV_LORA, MLA_H, V_D), KV_LORA ** -0.5),
        'mla_qn': gain((N_EVEN, QK_D)),
        'mla_kn': gain((N_EVEN, QK_D)),
        'o_w_in': nrm((N_ODD, D_MODEL, sum(ODD_SIZES)), D_MODEL ** -0.5),
        'o_w_out': nrm((N_ODD, ODD_OUT, D_MODEL), ODD_OUT ** -0.5),
        'moba_qn': gain((N_ODD, HD)),
        'moba_kn': gain((N_ODD, HD)),
    }


def reference(x_prompt, x_sample, cache_nsa_cmp_k, cache_nsa_cmp_v, cache_nsa_sel_k, cache_nsa_sel_v,
              state_nsa_win_k, state_nsa_win_v, cache_mla_ckv, cache_mla_krope, cache_moba_k, cache_moba_v,
              cache_mem_k, cache_mem_v, page_table, mem_prompt,
              rel_bias, ln_g, mem_norm_g, mem_wkv, mem_qn, mem_kn,
              e_w_in, e_w_out, nsa_qn, nsa_kn, nsa_cmp_pe, nsa_cmp_w1, nsa_cmp_w2,
              mla_cqn, mla_wuq, mla_ckvn, mla_wuk, mla_wuv, mla_qn, mla_kn,
              o_w_in, o_w_out, moba_qn, moba_kn):
    xp, xs = x_prompt, x_sample
    even_caches = (cache_nsa_cmp_k, cache_nsa_cmp_v, cache_nsa_sel_k, cache_nsa_sel_v,
                   state_nsa_win_k, state_nsa_win_v, cache_mla_ckv, cache_mla_krope)
    even_p, even_s, odd_p, odd_s, mem_k_p, mem_v_p = [], [], [], [], [], []
    for i in range(DEPTH):
        km_p, vm_p = mem_kv(mem_prompt, mem_norm_g[i], mem_wkv[i], mem_kn[i])
        mem_k_p.append(km_p)
        mem_v_p.append(vm_p)
        li = i // 2
        if i % 2 == 0:
            W = {'w_in': e_w_in[li], 'w_out': e_w_out[li], 'nsa_qn': nsa_qn[li], 'nsa_kn': nsa_kn[li],
                 'cmp_pe': nsa_cmp_pe[li], 'cmp_w1': nsa_cmp_w1[li], 'cmp_w2': nsa_cmp_w2[li],
                 'mla_cqn': mla_cqn[li], 'mla_wuq': mla_wuq[li], 'mla_ckvn': mla_ckvn[li],
                 'mla_wuk': mla_wuk[li], 'mla_wuv': mla_wuv[li], 'mla_qn': mla_qn[li], 'mla_kn': mla_kn[li],
                 'mem_qn': mem_qn[i]}
            dp, st_p = even_prompt(xp, ln_g[i], W, km_p, vm_p, rel_bias)
            ds, st_s = even_sample(xs, ln_g[i], W, cache_mem_k[i], cache_mem_v[i], rel_bias, even_caches, li, page_table)
            even_p.append(st_p)
            even_s.append(st_s)
        else:
            W = {'w_in': o_w_in[li], 'w_out': o_w_out[li], 'moba_qn': moba_qn[li], 'moba_kn': moba_kn[li],
                 'mem_qn': mem_qn[i]}
            dp, st_p = odd_prompt(xp, ln_g[i], W, km_p, vm_p, rel_bias)
            ds, st_s = odd_sample(xs, ln_g[i], W, cache_mem_k[i], cache_mem_v[i], rel_bias,
                                  cache_moba_k, cache_moba_v, li, page_table)
            odd_p.append(st_p)
            odd_s.append(st_s)
        xp = xp + dp
        xs = xs + ds

    def stk(lst, j):
        return jnp.stack([t[j] for t in lst])

    return (xp, xs,
            stk(even_p, 0), stk(even_p, 1), stk(even_p, 2), stk(even_p, 3),
            stk(even_p, 4), stk(even_p, 5), stk(even_p, 6), stk(even_p, 7),
            stk(odd_p, 0), stk(odd_p, 1), jnp.stack(mem_k_p), jnp.stack(mem_v_p),
            stk(even_s, 0), stk(even_s, 1), stk(even_s, 2), stk(even_s, 3),
            stk(even_s, 4), stk(even_s, 5), stk(even_s, 6), stk(even_s, 7),
            stk(odd_s, 0), stk(odd_s, 1))
```

```python
import functools
import math

import jax
import jax.numpy as jnp
import numpy as np
from jax import lax
from jax.experimental import pallas as pl
from jax.experimental.pallas import tpu as pltpu

PAGE_SIZE = 128
HD = 64
N_BIAS_HEADS = 8
NSA_H = N_BIAS_HEADS
NSA_G = 2
NSA_HPG = NSA_H // NSA_G
CMP_STRIDE = 16
CMP_LEN = 2 * CMP_STRIDE
CMP_HID = 128
SEL_BLK = 64
SEL_TOPK = 16
NSA_WINDOW = 512
MLA_H = 4
Q_LORA = 256
KV_LORA = 128
NOPE_D = 64
ROPE_D = 32
QK_D = NOPE_D + ROPE_D
V_D = 64
ROPE_THETA = 10000.0
MOBA_H = N_BIAS_HEADS
MOBA_KVH = 2
MOBA_HPG = MOBA_H // MOBA_KVH
MOBA_BLK = 256
MOBA_TOPK = 3
MEM_H = 4
N_BUCKETS = 32
T5_MAX_DIST = 2048
Q_BLOCK = 128
EVEN_SIZES = (NSA_H * HD, 6 * NSA_G * HD, 3 * NSA_H, NSA_H * HD, Q_LORA, KV_LORA, ROPE_D, MLA_H * V_D, MEM_H * HD, MEM_H * HD)
ODD_SIZES = (MOBA_H * HD, 2 * MOBA_KVH * HD, MOBA_H * HD, MEM_H * HD, MEM_H * HD)
EPS = 1e-6
NEG = -1e30
FORCE = 1e9

VMEM_LIMIT_BYTES = 56 * 1024 * 1024
LANES = 128
ATT_TILE = 256
T5_NEAR_TILES = 8
assert (T5_NEAR_TILES - 1) * ATT_TILE - (ATT_TILE - 1) > 16 * 128 ** (15 / 16) + 16


def _mm_kernel(*refs, has_gain, has_res):
    it = iter(refs)
    x_ref = next(it)
    g_ref = next(it) if has_gain else None
    w_ref = next(it)
    r_ref = next(it) if has_res else None
    o_ref = next(it)
    x = x_ref[...]
    if has_gain:
        x = x * lax.rsqrt(jnp.mean(x * x, axis=-1, keepdims=True) + EPS) * g_ref[...]
    acc = jnp.dot(x.astype(jnp.bfloat16), w_ref[...], preferred_element_type=jnp.float32)
    if has_res:
        acc = acc + r_ref[...]
    o_ref[...] = acc


def _matmul(x, w, gain=None, residual=None):
    M, K = x.shape
    N = w.shape[1]
    n_pad = -(-N // LANES) * LANES
    wb = jnp.pad(w, ((0, 0), (0, n_pad - N))).astype(jnp.bfloat16)
    tm = min(512, M)
    assert M % tm == 0 and tm % 8 == 0
    args = [x]
    specs = [pl.BlockSpec((tm, K), lambda i: (i, 0))]
    if gain is not None:
        args.append(gain.reshape(1, K).astype(jnp.float32))
        specs.append(pl.BlockSpec((1, K), lambda i: (0, 0)))
    args.append(wb)
    specs.append(pl.BlockSpec((K, n_pad), lambda i: (0, 0)))
    if residual is not None:
        assert n_pad == N
        args.append(residual)
        specs.append(pl.BlockSpec((tm, n_pad), lambda i: (i, 0)))
    out = pl.pallas_call(
        functools.partial(_mm_kernel, has_gain=gain is not None, has_res=residual is not None),
        grid=(M // tm,),
        in_specs=specs,
        out_specs=pl.BlockSpec((tm, n_pad), lambda i: (i, 0)),
        out_shape=jax.ShapeDtypeStruct((M, n_pad), jnp.float32),
        compiler_params=pltpu.CompilerParams(dimension_semantics=("arbitrary",), vmem_limit_bytes=VMEM_LIMIT_BYTES),
        name="rmsnorm_matmul",
    )(*args)
    return out[:, :N] if n_pad != N else out


def _flash_kernel(*refs, R, RK, RM, t, nk, nd, mode):
    it = iter(refs)
    q_ref, k_ref, v_ref = next(it), next(it), next(it)
    nm_ref = e_ref = bias_ref = None
    if RM:
        nm_ref, e_ref = next(it), next(it)
    if mode != 'full':
        bias_ref = next(it)
    o_ref, m_sc, l_sc, acc_sc = next(it), next(it), next(it), next(it)
    qt = pl.program_id(2)
    dk = q_ref.shape[-1]
    q = q_ref[0, 0].reshape(R * t, dk)
    m_sc[...] = jnp.full(m_sc.shape, NEG, jnp.float32)
    l_sc[...] = jnp.zeros(l_sc.shape, jnp.float32)
    acc_sc[...] = jnp.zeros(acc_sc.shape, jnp.float32)
    nt = (((1,), (1,)), ((), ()))

    def step(kt, with_bias):
        ks = pl.ds(pl.multiple_of(kt * t, t), t)
        if RK == 1:
            s = lax.dot_general(q, k_ref[0, 0, 0, ks, :], nt, preferred_element_type=jnp.float32)
        else:
            s = jnp.concatenate(
                [lax.dot_general(q[r * t:(r + 1) * t], k_ref[0, 0, r, ks, :], nt, preferred_element_type=jnp.float32)
                 for r in range(R)], axis=0)
        if RM:
            nb = nm_ref.shape[-1]
            mexp = jnp.dot(nm_ref[0, 0].reshape(RM * t, nb), e_ref[kt], preferred_element_type=jnp.float32)
            if RM == R:
                s = s + mexp
            else:
                s = (s.reshape(R, t, t) + mexp[None]).reshape(R * t, t)
        if with_bias:
            s = s + bias_ref[0, qt - kt]
        m_prev = m_sc[...]
        m_new = jnp.maximum(m_prev, jnp.max(s, axis=-1, keepdims=True))
        alpha = jnp.exp(m_prev - m_new)
        p = jnp.exp(s - m_new)
        l_sc[...] = alpha * l_sc[...] + jnp.sum(p, axis=-1, keepdims=True)
        acc_sc[...] = alpha * acc_sc[...] + jnp.dot(p.astype(jnp.bfloat16), v_ref[0, 0, ks, :],
                                                     preferred_element_type=jnp.float32)
        m_sc[...] = m_new

    def loop(lo, hi, with_bias):
        def body(kt, c):
            step(kt, with_bias)
            return c
        lax.fori_loop(lo, hi, body, 0)

    if mode == 'full':
        for kt in range(nk):
            step(kt, False)
    else:
        near_lo = jnp.maximum(qt - (nd - 1), 0)
        if mode == 'causal':
            loop(0, near_lo, False)
        loop(near_lo, qt + 1, True)
    o_ref[0, 0] = (acc_sc[...] / l_sc[...]).reshape(o_ref.shape[2:])


def _flash(q, k, v, negmask=None, expand=None, bias=None, *, mode, name):
    B, G, R, T, dk = q.shape
    RK, Tk = k.shape[2], k.shape[3]
    dv = v.shape[-1]
    t = min(ATT_TILE, T)
    assert T % t == 0 and Tk % t == 0 and (mode == 'full' or Tk == T)
    RM = 0 if negmask is None else negmask.shape[2]
    nd = 0 if bias is None else bias.shape[1]
    args = [q, k, v]
    specs = [pl.BlockSpec((1, 1, R, t, dk), lambda b, g, i: (b, g, 0, i, 0)),
             pl.BlockSpec((1, 1, RK, Tk, dk), lambda b, g, i: (b, g, 0, 0, 0)),
             pl.BlockSpec((1, 1, Tk, dv), lambda b, g, i: (b, g, 0, 0))]
    if RM:
        nb = negmask.shape[-1]
        args += [negmask, expand]
        specs += [pl.BlockSpec((1, 1, RM, t, nb), lambda b, g, i: (b, g, 0, i, 0)),
                  pl.BlockSpec(expand.shape, lambda b, g, i: (0, 0, 0))]
    if mode != 'full':
        args.append(bias)
        specs.append(pl.BlockSpec((1, nd, R * t, t), lambda b, g, i: (g, 0, 0, 0)))
    return pl.pallas_call(
        functools.partial(_flash_kernel, R=R, RK=RK, RM=RM, t=t, nk=Tk // t, nd=nd, mode=mode),
        grid=(B, G, T // t),
        in_specs=specs,
        out_specs=pl.BlockSpec((1, 1, R, t, dv), lambda b, g, i: (b, g, 0, i, 0)),
        out_shape=jax.ShapeDtypeStruct((B, G, R, T, dv), jnp.float32),
        scratch_shapes=[pltpu.VMEM((R * t, 1), jnp.float32), pltpu.VMEM((R * t, 1), jnp.float32),
                        pltpu.VMEM((R * t, dv), jnp.float32)],
        compiler_params=pltpu.CompilerParams(dimension_semantics=("arbitrary", "arbitrary", "arbitrary"),
                                             vmem_limit_bytes=VMEM_LIMIT_BYTES),
        name=name,
    )(*args)


def rms_norm(x, g):
    xf = x.astype(jnp.float32)
    y = xf * lax.rsqrt(jnp.mean(xf * xf, axis=-1, keepdims=True) + EPS)
    return (y * g.astype(jnp.float32)).astype(x.dtype)


def split_cols(h, sizes):
    return jnp.split(h, np.cumsum(sizes)[:-1].tolist(), axis=-1)


def masked_softmax(s, mask):
    p = jax.nn.softmax(jnp.where(mask, s, NEG), axis=-1)
    return p * mask


def t5_bucket(dist):
    dist = jnp.maximum(dist, 0)
    exact = N_BUCKETS // 2
    far = exact + (jnp.log(jnp.maximum(dist, 1).astype(jnp.float32) / exact)
                   / math.log(T5_MAX_DIST / exact) * (N_BUCKETS - exact)).astype(jnp.int32)
    return jnp.where(dist < exact, dist, jnp.minimum(far, N_BUCKETS - 1))


def rope(x, pos):
    half = ROPE_D // 2
    inv = ROPE_THETA ** (-jnp.arange(half, dtype=jnp.float32) / half)
    ang = pos.astype(jnp.float32)[:, None] * inv
    ang = ang.reshape((ang.shape[0],) + (1,) * (x.ndim - 3) + (half,))
    cos, sin = jnp.cos(ang), jnp.sin(ang)
    xf = x.astype(jnp.float32)
    x1, x2 = xf[..., :half], xf[..., half:]
    return jnp.concatenate([x1 * cos - x2 * sin, x2 * cos + x1 * sin], axis=-1).astype(x.dtype)


def map_query_blocks(fn, n_q, qb):
    starts = jnp.arange(n_q // qb, dtype=jnp.int32) * qb
    out = jnp.moveaxis(lax.map(fn, starts), 0, 1)
    return out.reshape((out.shape[0], n_q) + out.shape[3:])


def paged_rows(pool, li, page_table, new):
    past = page_table.shape[1] * PAGE_SIZE
    s_new = new.shape[1]

    def get(pos, hidx):
        b = jnp.arange(new.shape[0]).reshape((-1,) + (1,) * (pos.ndim - 1))
        pp = jnp.clip(pos, 0, past - 1)
        old = pool[li, page_table[b, pp // PAGE_SIZE], pp % PAGE_SIZE, hidx]
        cur = new[b, jnp.clip(pos - past, 0, s_new - 1), hidx]
        return jnp.where((pos < past)[..., None], old, cur)
    return get


def gather_past(pool, li, page_table):
    b, n_pages = page_table.shape
    return pool[li, page_table].reshape((b, n_pages * PAGE_SIZE) + pool.shape[3:])


def _bias_tiles(rel_bias, heads_per_group, t, nd, window=None):
    i = jnp.arange(t)[:, None]
    j = jnp.arange(t)[None, :]
    dist = jnp.arange(nd)[:, None, None] * t + i - j
    b = rel_bias[t5_bucket(dist)] - rel_bias[N_BUCKETS - 1]
    ok = dist >= 0
    if window is not None:
        ok = ok & (dist < window)
    b = jnp.where(ok[..., None], b, NEG)
    n_heads = rel_bias.shape[1]
    b = jnp.transpose(b, (3, 0, 1, 2)).reshape(n_heads // heads_per_group, heads_per_group, nd, t, t)
    return jnp.transpose(b, (0, 2, 1, 3, 4)).reshape(n_heads // heads_per_group, nd, heads_per_group * t, t).astype(jnp.float32)


def _causal_tiles(n_groups, rows, t):
    ok = jnp.arange(t)[:, None] >= jnp.arange(t)[None, :]
    b = jnp.where(ok, 0.0, NEG).astype(jnp.float32)
    return jnp.broadcast_to(jnp.tile(b, (rows, 1))[None, None], (n_groups, 1, rows * t, t))


def _expand_onehot(n_keys, blk, nb, t):
    key_blk = (jnp.arange(n_keys) // blk).reshape(n_keys // t, 1, t)
    return (key_blk == jnp.arange(nb)[None, :, None]).astype(jnp.bfloat16)


def _heads_first(x, groups):
    B, T, H, D = x.shape
    return jnp.transpose(x.reshape(B, T, groups, H // groups, D), (0, 2, 3, 1, 4))


def _heads_last(o):
    B, G, R, T, D = o.shape
    return jnp.transpose(o, (0, 3, 1, 2, 4)).reshape(B, T, G * R, D)


def cmp_chunk_proj(raw, w1):
    B, L = raw.shape[:2]
    ch = raw.reshape(B, L // CMP_STRIDE, CMP_STRIDE, NSA_G, HD)
    a = jnp.einsum('bnsgd,sdh->bngh', ch, w1[:CMP_STRIDE])
    b = jnp.einsum('bnsgd,sdh->bngh', ch, w1[CMP_STRIDE:])
    return a, b


def cmp_finish(a, b, pe, w1, w2):
    pe_h = jnp.einsum('sd,sdh->h', pe, w1)
    h = jax.nn.gelu(a[:, :-1] + b[:, 1:] + pe_h)
    return jnp.einsum('bngh,hd->bngd', h, w2)


def nsa_compress(raw_k_parts, raw_v_parts, W):
    out = []
    for j, parts in enumerate((raw_k_parts, raw_v_parts)):
        w1 = W['cmp_w1'][j]
        ab = [cmp_chunk_proj(r, w1) for r in parts]
        a = jnp.concatenate([t[0] for t in ab], axis=1)
        b = jnp.concatenate([t[1] for t in ab], axis=1)
        out.append(cmp_finish(a, b, W['cmp_pe'][j], w1, W['cmp_w2'][j]))
    return rms_norm(out[0], W['nsa_kn'][0]), out[1]


def sel_block_scores(imp, nsb):
    r = SEL_BLK // CMP_STRIDE
    front = CMP_LEN // CMP_STRIDE - 1
    nc = imp.shape[-1]
    imp = jnp.pad(imp, [(0, 0)] * (imp.ndim - 1) + [(front, r * nsb - nc)])
    score = 0.0
    for o in range(-front, r):
        lo = o * CMP_STRIDE
        w = max(0, min(lo + CMP_LEN, SEL_BLK) - max(lo, 0)) / CMP_LEN
        start = o + front
        score = score + w * imp[..., start:start + r * (nsb - 1) + 1:r]
    return score


def _nsa_compressed_and_select(q, q_pos, kc, vc, nsb, bias_t):
    B, Q = q.shape[:2]
    scale = HD ** -0.5
    f32 = jnp.float32
    qg = q.reshape(B, Q, NSA_G, NSA_HPG, HD)
    tg = bias_t.T.reshape(NSA_G, NSA_HPG, N_BUCKETS)
    nc = kc.shape[1]
    dist_c = q_pos[:, None] - (jnp.arange(nc) * CMP_STRIDE + CMP_LEN - 1)[None, :]
    bias_c = jnp.transpose(tg[:, :, t5_bucket(dist_c)], (2, 0, 1, 3))[None]
    s_c = jnp.einsum('bqgpd,bngd->bqgpn', qg, kc, preferred_element_type=f32) * scale + bias_c
    p_c = masked_softmax(s_c, (dist_c >= 0)[None, :, None, None, :])
    o_c = jnp.einsum('bqgpn,bngd->bqgpd', p_c.astype(vc.dtype), vc)
    score = sel_block_scores(p_c.sum(axis=3), nsb)
    blk = jnp.arange(nsb)[None, :]
    own = (q_pos // SEL_BLK)[:, None]
    forced = (blk == 0) | (blk == own) | (blk == own - 1)
    score = jnp.where(forced[None, :, None, :], FORCE,
                      jnp.where((blk <= own)[None, :, None, :], score, -jnp.inf))
    k_eff = min(SEL_TOPK, nsb)
    _, idx = lax.top_k(score, k_eff)
    return o_c, idx


def nsa_block(q, gates, q_pos, kc, vc, get_k, get_v, nsb, kw, vw, kw_pos, bias_t):
    B, Q = q.shape[:2]
    scale = HD ** -0.5
    f32 = jnp.float32
    qg = q.reshape(B, Q, NSA_G, NSA_HPG, HD)
    tg = bias_t.T.reshape(NSA_G, NSA_HPG, N_BUCKETS)
    o_c, idx = _nsa_compressed_and_select(q, q_pos, kc, vc, nsb, bias_t)
    k_eff = idx.shape[-1]
    pos_s = (idx[..., None] * SEL_BLK + jnp.arange(SEL_BLK)).reshape(B, Q, NSA_G, k_eff * SEL_BLK)
    gidx = jnp.arange(NSA_G)[None, None, :, None]
    k_s, v_s = get_k(pos_s, gidx), get_v(pos_s, gidx)
    dist_s = q_pos[None, :, None, None] - pos_s
    bias_s = tg[gidx[..., None], jnp.arange(NSA_HPG)[None, None, None, :, None], t5_bucket(dist_s)[:, :, :, None, :]]
    s_s = jnp.einsum('bqgpd,bqgrd->bqgpr', qg, k_s, preferred_element_type=f32) * scale + bias_s
    p_s = masked_softmax(s_s, (dist_s >= 0)[:, :, :, None, :])
    o_s = jnp.einsum('bqgpr,bqgrd->bqgpd', p_s.astype(v_s.dtype), v_s)
    dist_w = q_pos[:, None] - kw_pos[None, :]
    mask_w = (dist_w >= 0) & (dist_w < NSA_WINDOW) & (kw_pos >= 0)[None, :]
    bias_w = jnp.transpose(tg[:, :, t5_bucket(dist_w)], (2, 0, 1, 3))[None]
    s_w = jnp.einsum('bqgpd,blgd->bqgpl', qg, kw, preferred_element_type=f32) * scale + bias_w
    p_w = masked_softmax(s_w, mask_w[None, :, None, None, :])
    o_w = jnp.einsum('bqgpl,blgd->bqgpd', p_w.astype(vw.dtype), vw)
    g = gates.reshape(B, Q, NSA_G, NSA_HPG, 3, 1)
    o = g[..., 0, :] * o_c + g[..., 1, :] * o_s + g[..., 2, :] * o_w
    return o.reshape(B, Q, NSA_H * HD)


def mla_keys(ckv, kr, k_pos, W):
    kn = jnp.einsum('blc,chd->blhd', ckv, W['mla_wuk'])
    krf = kr.astype(jnp.float32)
    ss = jnp.einsum('blhd,blhd->blh', kn, kn, preferred_element_type=jnp.float32) + jnp.sum(krf * krf, -1)[..., None]
    inv = lax.rsqrt(ss / QK_D + EPS)
    krr = rope(kr * W['mla_kn'][NOPE_D:], k_pos)
    return kn, inv, krr


def mla_block(q_nope, q_rope, q_pos, keys, ckv, k_pos, W):
    kn, inv, krr = keys
    B, Q = q_nope.shape[:2]
    s = (jnp.einsum('bqhd,blhd->bhql', q_nope * W['mla_kn'][:NOPE_D], kn, preferred_element_type=jnp.float32)
         + jnp.einsum('bqhd,bld->bhql', q_rope, krr, preferred_element_type=jnp.float32))
    s = s * jnp.transpose(inv, (0, 2, 1))[:, :, None, :] * QK_D ** -0.5
    p = masked_softmax(s, (k_pos[None, :] <= q_pos[:, None])[None, None])
    o_lat = jnp.einsum('bhql,blc->bqhc', p.astype(ckv.dtype), ckv)
    return jnp.einsum('bqhc,chd->bqhd', o_lat, W['mla_wuv']).reshape(B, Q, MLA_H * V_D)


def _mla_prompt(P, pos, W):
    kn, inv, krr = mla_keys(P['ckv'], P['kr'], pos, W)
    B, T = kn.shape[:2]
    kfull = jnp.concatenate([kn, jnp.broadcast_to(krr[:, :, None, :], (B, T, MLA_H, ROPE_D))], axis=-1)
    kfull = kfull * (inv * QK_D ** -0.5)[..., None]
    qfull = jnp.concatenate([P['q_nope'] * W['mla_kn'][:NOPE_D], P['q_rope']], axis=-1)
    padd = ((0, 0), (0, 0), (0, 0), (0, LANES - QK_D))
    kk = _heads_first(jnp.pad(kfull, padd), 1).astype(jnp.bfloat16)
    qq = _heads_first(jnp.pad(qfull, padd), 1).astype(jnp.bfloat16)
    vv = P['ckv'].astype(jnp.bfloat16)[:, None]
    t = min(ATT_TILE, T)
    o_lat = _flash(qq, kk, vv, bias=_causal_tiles(1, MLA_H, t), mode='causal', name="mla_prompt")
    o_lat = _heads_last(o_lat)
    return jnp.einsum('bqhc,chd->bqhd', o_lat, W['mla_wuv']).reshape(B, T, MLA_H * V_D)


def moba_means_paged(pool_k, li, page_table, k_new):
    B, n_pages = page_table.shape
    past = n_pages * PAGE_SIZE
    total = past + k_new.shape[1]
    nbm = -(-total // MOBA_BLK)
    pps = MOBA_BLK // PAGE_SIZE
    page_sum = pool_k[li, page_table].astype(jnp.float32).sum(axis=2)
    npp = -(-n_pages // pps) * pps
    page_sum = jnp.pad(page_sum, ((0, 0), (0, npp - n_pages), (0, 0), (0, 0)))
    blk_sum = page_sum.reshape(B, npp // pps, pps, MOBA_KVH, HD).sum(axis=2)
    blk_sum = jnp.pad(blk_sum, ((0, 0), (0, nbm - npp // pps), (0, 0), (0, 0)))
    new_blk = (past + jnp.arange(k_new.shape[1])) // MOBA_BLK
    onehot = (new_blk[:, None] == jnp.arange(nbm)[None, :]).astype(jnp.float32)
    blk_sum = blk_sum + jnp.einsum('bsgd,sn->bngd', k_new.astype(jnp.float32), onehot)
    return blk_sum / MOBA_BLK


def moba_block(q, q_pos, kmean_h, get_k, get_v, bias_t):
    B, Q = q.shape[:2]
    nbm = kmean_h.shape[1]
    own = q_pos // MOBA_BLK
    own_pos = own[:, None] * MOBA_BLK + jnp.arange(MOBA_BLK)
    shp = (B, Q, MOBA_H, MOBA_BLK)
    pos = jnp.broadcast_to(own_pos[None, :, None, :], shp)
    ok = jnp.ones(shp, bool)
    k_eff = min(MOBA_TOPK, nbm - 1)
    if k_eff > 0:
        gs = jnp.einsum('bqhd,bnhd->bqhn', q, kmean_h, preferred_element_type=jnp.float32)
        full_past = jnp.arange(nbm)[None, :] < own[:, None]
        gs = jnp.where(full_past[None, :, None, :], gs, -jnp.inf)
        _, idx = lax.top_k(gs, k_eff)
        pos_sel = (idx[..., None] * MOBA_BLK + jnp.arange(MOBA_BLK)).reshape(B, Q, MOBA_H, k_eff * MOBA_BLK)
        ok_sel = jnp.repeat(idx < own[None, :, None, None], MOBA_BLK, axis=-1)
        pos = jnp.concatenate([pos_sel, pos], axis=-1)
        ok = jnp.concatenate([ok_sel, ok], axis=-1)
    hidx = (jnp.arange(MOBA_H) // MOBA_HPG)[None, None, :, None]
    k, v = get_k(pos, hidx), get_v(pos, hidx)
    dist = q_pos[None, :, None, None] - pos
    bias = bias_t.T[jnp.arange(MOBA_H)[None, None, :, None], t5_bucket(dist)]
    s = jnp.einsum('bqhd,bqhrd->bqhr', q, k, preferred_element_type=jnp.float32) * HD ** -0.5 + bias
    p = masked_softmax(s, ok & (dist >= 0))
    o = jnp.einsum('bqhr,bqhrd->bqhd', p.astype(v.dtype), v)
    return o.reshape(B, Q, MOBA_H * HD)


def mem_kv(mem, g, wkv, kn):
    B, N = mem.shape[:2]
    h = _matmul(mem.reshape(B * N, -1), wkv, gain=g).reshape(B, N, 2, MEM_H, HD)
    return rms_norm(h[:, :, 0], kn), h[:, :, 1]


def mem_attend(q, km, vm):
    B, Q = q.shape[:2]
    s = jnp.einsum('bqhd,bmhd->bhqm', q, km, preferred_element_type=jnp.float32) * HD ** -0.5
    p = jax.nn.softmax(s, axis=-1).astype(vm.dtype)
    return jnp.einsum('bhqm,bmhd->bqhd', p, vm).reshape(B, Q, MEM_H * HD)


def _mem_attend_prompt(q, km, vm):
    B, T = q.shape[:2]
    qq = _heads_first(q * HD ** -0.5, MEM_H).astype(jnp.bfloat16)
    kk = _heads_first(km, MEM_H).astype(jnp.bfloat16)
    vv = _heads_first(vm, MEM_H)[:, :, 0].astype(jnp.bfloat16)
    return _heads_last(_flash(qq, kk, vv, mode='full', name="mem_prompt")).reshape(B, T, MEM_H * HD)


def even_project(x, ln, W, pos):
    B, T = x.shape[:2]
    h = _matmul(x.reshape(B * T, -1), W['w_in'], gain=ln).reshape(B, T, -1)
    nq, nkv, ngt, nz, cq, ckv, kr, mz, mq, memz = split_cols(h, EVEN_SIZES)
    kv = nkv.reshape(B, T, 6, NSA_G, HD)
    qm = _matmul(cq.reshape(B * T, -1), W['mla_wuq'], gain=W['mla_cqn']).reshape(B, T, MLA_H, QK_D)
    qm = rms_norm(qm, W['mla_qn'])
    return {
        'q': rms_norm(nq.reshape(B, T, NSA_H, HD), W['nsa_qn']),
        'gates': jax.nn.sigmoid(ngt.reshape(B, T, NSA_H, 3)),
        'raw_kc': kv[:, :, 0], 'raw_vc': kv[:, :, 1],
        'k_sel': rms_norm(kv[:, :, 2], W['nsa_kn'][1]), 'v_sel': kv[:, :, 3],
        'k_win': rms_norm(kv[:, :, 4], W['nsa_kn'][2]), 'v_win': kv[:, :, 5],
        'q_nope': qm[..., :NOPE_D], 'q_rope': rope(qm[..., NOPE_D:], pos),
        'ckv': rms_norm(ckv, W['mla_ckvn']), 'kr': kr,
        'qmem': rms_norm(mq.reshape(B, T, MEM_H, HD), W['mem_qn']),
        'z': (nz, mz, memz),
    }


def even_finish(x, P, o_nsa, o_mla, o_mem, W):
    z_nsa, z_mla, z_mem = P['z']
    mixed = jnp.concatenate([o_nsa * jax.nn.silu(z_nsa), o_mla * jax.nn.silu(z_mla), o_mem * jax.nn.silu(z_mem)], axis=-1)
    B, T, D = x.shape
    return _matmul(mixed.reshape(B * T, -1), W['w_out'], residual=x.reshape(B * T, D)).reshape(B, T, D)


def _nsa_prompt(P, kc, vc, bias_t):
    q = P['q']
    B, T = q.shape[:2]
    nsb = T // SEL_BLK
    t = min(ATT_TILE, T)
    o_c, idx = _nsa_compressed_and_select(q, jnp.arange(T), kc, vc, nsb, bias_t)
    sel = jnp.any(idx[..., None] == jnp.arange(nsb), axis=-2)
    sel = sel & (jnp.arange(nsb)[None, :] <= (jnp.arange(T) // SEL_BLK)[:, None])[None, :, None, :]
    negmask = jnp.where(sel, 0.0, NEG).astype(jnp.bfloat16)
    negmask = jnp.transpose(negmask, (0, 2, 1, 3))[:, :, None]
    qq = _heads_first(q * HD ** -0.5, NSA_G).astype(jnp.bfloat16)
    o_s = _flash(qq, _heads_first(P['k_sel'], NSA_G).astype(jnp.bfloat16),
                 _heads_first(P['v_sel'], NSA_G)[:, :, 0].astype(jnp.bfloat16),
                 negmask, _expand_onehot(T, SEL_BLK, nsb, t),
                 _bias_tiles(bias_t, NSA_HPG, t, min(T5_NEAR_TILES, T // t)), mode='causal', name="nsa_selected_prompt")
    n_win = -(-(NSA_WINDOW - 1) // t) + 1
    o_w = _flash(qq, _heads_first(P['k_win'], NSA_G).astype(jnp.bfloat16),
                 _heads_first(P['v_win'], NSA_G)[:, :, 0].astype(jnp.bfloat16),
                 bias=_bias_tiles(bias_t, NSA_HPG, t, min(n_win, T // t), window=NSA_WINDOW),
                 mode='window', name="nsa_window_prompt")
    g = P['gates'].reshape(B, T, NSA_G, NSA_HPG, 3, 1)
    o_s = _heads_last(o_s).reshape(B, T, NSA_G, NSA_HPG, HD)
    o_w = _heads_last(o_w).reshape(B, T, NSA_G, NSA_HPG, HD)
    o = g[..., 0, :] * o_c + g[..., 1, :] * o_s + g[..., 2, :] * o_w
    return o.reshape(B, T, NSA_H * HD)


def even_prompt(x, ln, W, km, vm, bias_t):
    B, T = x.shape[:2]
    pos = jnp.arange(T, dtype=jnp.int32)
    P = even_project(x, ln, W, pos)
    kc, vc = nsa_compress([P['raw_kc']], [P['raw_vc']], W)
    o_nsa = _nsa_prompt(P, kc, vc, bias_t)
    o_mla = _mla_prompt(P, pos, W)
    o_mem = _mem_attend_prompt(P['qmem'], km, vm)
    y = even_finish(x, P, o_nsa, o_mla, o_mem, W)
    wk = min(NSA_WINDOW, T)
    state = (P['raw_kc'], P['raw_vc'], P['k_sel'], P['v_sel'], P['k_win'][:, T - wk:], P['v_win'][:, T - wk:], P['ckv'], P['kr'])
    return y, state


def even_sample(x, ln, W, km, vm, bias_t, caches, li, page_table):
    c_cmp_k, c_cmp_v, c_sel_k, c_sel_v, s_win_k, s_win_v, c_ckv, c_kr = caches
    B, S = x.shape[:2]
    past = page_table.shape[1] * PAGE_SIZE
    total = past + S
    pos = past + jnp.arange(S, dtype=jnp.int32)
    P = even_project(x, ln, W, pos)
    l_pad = -(-total // SEL_BLK) * SEL_BLK
    padn = ((0, 0), (0, l_pad - total), (0, 0), (0, 0))
    kc, vc = nsa_compress([gather_past(c_cmp_k, li, page_table), jnp.pad(P['raw_kc'], padn)],
                          [gather_past(c_cmp_v, li, page_table), jnp.pad(P['raw_vc'], padn)], W)
    nsb = l_pad // SEL_BLK
    get_k = paged_rows(c_sel_k, li, page_table, P['k_sel'])
    get_v = paged_rows(c_sel_v, li, page_table, P['v_sel'])
    kw = jnp.concatenate([s_win_k[li], P['k_win']], axis=1)
    vw = jnp.concatenate([s_win_v[li], P['v_win']], axis=1)
    wb = s_win_k.shape[2]
    kw_pos = past - wb + jnp.arange(wb + S)

    def nsa_fn(c):
        sl = lambda a: lax.dynamic_slice_in_dim(a, c, 1, 1)
        return nsa_block(sl(P['q']), sl(P['gates']), past + c + jnp.arange(1), kc, vc, get_k, get_v, nsb,
                         kw, vw, kw_pos, bias_t)
    o_nsa = map_query_blocks(nsa_fn, S, 1)
    ckv_all = jnp.concatenate([gather_past(c_ckv, li, page_table), P['ckv']], axis=1)
    kr_all = jnp.concatenate([gather_past(c_kr, li, page_table), P['kr']], axis=1)
    k_pos = jnp.arange(total, dtype=jnp.int32)
    o_mla = mla_block(P['q_nope'], P['q_rope'], pos, mla_keys(ckv_all, kr_all, k_pos, W), ckv_all, k_pos, W)
    o_mem = mem_attend(P['qmem'], km, vm)
    y = even_finish(x, P, o_nsa, o_mla, o_mem, W)
    wk = min(NSA_WINDOW, total)
    n_w = kw.shape[1]
    state = (P['raw_kc'], P['raw_vc'], P['k_sel'], P['v_sel'], kw[:, n_w - wk:], vw[:, n_w - wk:], P['ckv'], P['kr'])
    return y, state


def odd_project(x, ln, W):
    B, T = x.shape[:2]
    h = _matmul(x.reshape(B * T, -1), W['w_in'], gain=ln).reshape(B, T, -1)
    mq, mkv, mz, memq, memz = split_cols(h, ODD_SIZES)
    kv = mkv.reshape(B, T, 2, MOBA_KVH, HD)
    return {
        'q': rms_norm(mq.reshape(B, T, MOBA_H, HD), W['moba_qn']),
        'k': rms_norm(kv[:, :, 0], W['moba_kn']), 'v': kv[:, :, 1],
        'qmem': rms_norm(memq.reshape(B, T, MEM_H, HD), W['mem_qn']),
        'z': (mz, memz),
    }


def odd_finish(x, P, o_moba, o_mem, W):
    z_moba, z_mem = P['z']
    mixed = jnp.concatenate([o_moba * jax.nn.silu(z_moba), o_mem * jax.nn.silu(z_mem)], axis=-1)
    B, T, D = x.shape
    return _matmul(mixed.reshape(B * T, -1), W['w_out'], residual=x.reshape(B * T, D)).reshape(B, T, D)


def _moba_prompt(P, bias_t):
    q, k, v = P['q'], P['k'], P['v']
    B, T = q.shape[:2]
    assert T % MOBA_BLK == 0 and ATT_TILE == MOBA_BLK
    nbm = T // MOBA_BLK
    t = min(ATT_TILE, T)
    own = jnp.arange(T) // MOBA_BLK
    blk = jnp.arange(nbm)
    sel = jnp.broadcast_to((blk[None, :] == own[:, None])[None, :, None, :], (B, T, MOBA_H, nbm))
    k_eff = min(MOBA_TOPK, nbm - 1)
    if k_eff > 0:
        kmean = k.astype(jnp.float32).reshape(B, nbm, MOBA_BLK, MOBA_KVH, HD).sum(axis=2) / MOBA_BLK
        kmean_h = jnp.repeat(kmean, MOBA_HPG, axis=2)
        gs = jnp.einsum('bqhd,bnhd->bqhn', q, kmean_h, preferred_element_type=jnp.float32)
        full_past = blk[None, :] < own[:, None]
        gs = jnp.where(full_past[None, :, None, :], gs, -jnp.inf)
        _, idx = lax.top_k(gs, k_eff)
        picked = jnp.any(idx[..., None] == blk, axis=-2)
        sel = sel | (picked & full_past[None, :, None, :])
    negmask = jnp.where(sel, 0.0, NEG).astype(jnp.bfloat16)
    negmask = jnp.transpose(negmask.reshape(B, T, MOBA_KVH, MOBA_HPG, nbm), (0, 2, 3, 1, 4))
    qq = _heads_first(q * HD ** -0.5, MOBA_KVH).astype(jnp.bfloat16)
    o = _flash(qq, _heads_first(k, MOBA_KVH).astype(jnp.bfloat16), _heads_first(v, MOBA_KVH)[:, :, 0].astype(jnp.bfloat16),
               negmask, _expand_onehot(T, MOBA_BLK, nbm, t),
               _bias_tiles(bias_t, MOBA_HPG, t, min(T5_NEAR_TILES, T // t)), mode='causal', name="moba_prompt")
    return _heads_last(o).reshape(B, T, MOBA_H * HD)


def odd_prompt(x, ln, W, km, vm, bias_t):
    P = odd_project(x, ln, W)
    o_moba = _moba_prompt(P, bias_t)
    y = odd_finish(x, P, o_moba, _mem_attend_prompt(P['qmem'], km, vm), W)
    return y, (P['k'], P['v'])


def odd_sample(x, ln, W, km, vm, bias_t, c_k, c_v, li, page_table):
    B, S = x.shape[:2]
    past = page_table.shape[1] * PAGE_SIZE
    P = odd_project(x, ln, W)
    kmean_h = jnp.repeat(moba_means_paged(c_k, li, page_table, P['k']), MOBA_HPG, axis=2)
    get_k = paged_rows(c_k, li, page_table, P['k'])
    get_v = paged_rows(c_v, li, page_table, P['v'])

    def moba_fn(c):
        return moba_block(lax.dynamic_slice_in_dim(P['q'], c, 1, 1), past + c + jnp.arange(1), kmean_h, get_k, get_v, bias_t)
    o_moba = map_query_blocks(moba_fn, S, 1)
    y = odd_finish(x, P, o_moba, mem_attend(P['qmem'], km, vm), W)
    return y, (P['k'], P['v'])


def kernel(x_prompt, x_sample, cache_nsa_cmp_k, cache_nsa_cmp_v, cache_nsa_sel_k, cache_nsa_sel_v, state_nsa_win_k, state_nsa_win_v, cache_mla_ckv, cache_mla_krope, cache_moba_k, cache_moba_v, cache_mem_k, cache_mem_v, page_table, mem_prompt, rel_bias, ln_g, mem_norm_g, mem_wkv, mem_qn, mem_kn, e_w_in, e_w_out, nsa_qn, nsa_kn, nsa_cmp_pe, nsa_cmp_w1, nsa_cmp_w2, mla_cqn, mla_wuq, mla_ckvn, mla_wuk, mla_wuv, mla_qn, mla_kn, o_w_in, o_w_out, moba_qn, moba_kn):
    xp, xs = x_prompt, x_sample
    depth = ln_g.shape[0]
    even_caches = (cache_nsa_cmp_k, cache_nsa_cmp_v, cache_nsa_sel_k, cache_nsa_sel_v,
                   state_nsa_win_k, state_nsa_win_v, cache_mla_ckv, cache_mla_krope)
    even_p, even_s, odd_p, odd_s, mem_k_p, mem_v_p = [], [], [], [], [], []
    for i in range(depth):
        km_p, vm_p = mem_kv(mem_prompt, mem_norm_g[i], mem_wkv[i], mem_kn[i])
        mem_k_p.append(km_p)
        mem_v_p.append(vm_p)
        li = i // 2
        if i % 2 == 0:
            W = {'w_in': e_w_in[li], 'w_out': e_w_out[li], 'nsa_qn': nsa_qn[li], 'nsa_kn': nsa_kn[li],
                 'cmp_pe': nsa_cmp_pe[li], 'cmp_w1': nsa_cmp_w1[li], 'cmp_w2': nsa_cmp_w2[li],
                 'mla_cqn': mla_cqn[li], 'mla_wuq': mla_wuq[li], 'mla_ckvn': mla_ckvn[li],
                 'mla_wuk': mla_wuk[li], 'mla_wuv': mla_wuv[li], 'mla_qn': mla_qn[li], 'mla_kn': mla_kn[li],
                 'mem_qn': mem_qn[i]}
            xp, st_p = even_prompt(xp, ln_g[i], W, km_p, vm_p, rel_bias)
            xs, st_s = even_sample(xs, ln_g[i], W, cache_mem_k[i], cache_mem_v[i], rel_bias, even_caches, li, page_table)
            even_p.append(st_p)
            even_s.append(st_s)
        else:
            W = {'w_in': o_w_in[li], 'w_out': o_w_out[li], 'moba_qn': moba_qn[li], 'moba_kn': moba_kn[li],
                 'mem_qn': mem_qn[i]}
            xp, st_p = odd_prompt(xp, ln_g[i], W, km_p, vm_p, rel_bias)
            xs, st_s = odd_sample(xs, ln_g[i], W, cache_mem_k[i], cache_mem_v[i], rel_bias,
                                  cache_moba_k, cache_moba_v, li, page_table)
            odd_p.append(st_p)
            odd_s.append(st_s)

    def stk(lst, j):
        return jnp.stack([t[j] for t in lst])

    return (xp, xs,
            stk(even_p, 0), stk(even_p, 1), stk(even_p, 2), stk(even_p, 3),
            stk(even_p, 4), stk(even_p, 5), stk(even_p, 6), stk(even_p, 7),
            stk(odd_p, 0), stk(odd_p, 1), jnp.stack(mem_k_p), jnp.stack(mem_v_p),
            stk(even_s, 0), stk(even_s, 1), stk(even_s, 2), stk(even_s, 3),
            stk(even_s, 4), stk(even_s, 5), stk(even_s, 6), stk(even_s, 7),
            stk(odd_s, 0), stk(odd_s, 1))
```

```python
import functools
import math

import jax
import jax.numpy as jnp
import numpy as np
from jax import lax
from jax.experimental import pallas as pl
from jax.experimental.pallas import tpu as pltpu

PAGE_SIZE = 128
HD = 64
N_BIAS_HEADS = 8
NSA_H = N_BIAS_HEADS
NSA_G = 2
NSA_HPG = NSA_H // NSA_G
CMP_STRIDE = 16
CMP_LEN = 2 * CMP_STRIDE
CMP_HID = 128
SEL_BLK = 64
SEL_TOPK = 16
NSA_WINDOW = 512
MLA_H = 4
Q_LORA = 256
KV_LORA = 128
NOPE_D = 64
ROPE_D = 32
QK_D = NOPE_D + ROPE_D
V_D = 64
ROPE_THETA = 10000.0
MOBA_H = N_BIAS_HEADS
MOBA_KVH = 2
MOBA_HPG = MOBA_H // MOBA_KVH
MOBA_BLK = 256
MOBA_TOPK = 3
MEM_H = 4
N_BUCKETS = 32
T5_MAX_DIST = 2048
Q_BLOCK = 128
EVEN_SIZES = (NSA_H * HD, 6 * NSA_G * HD, 3 * NSA_H, NSA_H * HD, Q_LORA, KV_LORA, ROPE_D, MLA_H * V_D, MEM_H * HD, MEM_H * HD)
ODD_SIZES = (MOBA_H * HD, 2 * MOBA_KVH * HD, MOBA_H * HD, MEM_H * HD, MEM_H * HD)
EPS = 1e-6
NEG = -1e30
FORCE = 1e9

VMEM_LIMIT_BYTES = 56 * 1024 * 1024
LANES = 128
ATT_TILE = 256
T5_SATURATED_DIST = 1536
assert T5_SATURATED_DIST > (N_BUCKETS // 2) * (T5_MAX_DIST / (N_BUCKETS // 2)) ** (15 / 16) + 16
T5_NEAR_TILES = 8
assert (T5_NEAR_TILES - 1) * ATT_TILE - (ATT_TILE - 1) >= T5_SATURATED_DIST


def _mm_kernel(*refs, has_gain, has_res):
    it = iter(refs)
    x_ref = next(it)
    g_ref = next(it) if has_gain else None
    w_ref = next(it)
    r_ref = next(it) if has_res else None
    o_ref = next(it)
    x = x_ref[...]
    if has_gain:
        x = x * lax.rsqrt(jnp.mean(x * x, axis=-1, keepdims=True) + EPS) * g_ref[...]
    acc = jnp.dot(x.astype(jnp.bfloat16), w_ref[...], preferred_element_type=jnp.float32)
    if has_res:
        acc = acc + r_ref[...]
    o_ref[...] = acc


def _matmul(x, w, gain=None, residual=None):
    M, K = x.shape
    N = w.shape[1]
    n_pad = -(-N // LANES) * LANES
    wb = jnp.pad(w, ((0, 0), (0, n_pad - N))).astype(jnp.bfloat16)
    tm = min(512, M)
    assert M % tm == 0 and tm % 8 == 0
    args = [x]
    specs = [pl.BlockSpec((tm, K), lambda i: (i, 0))]
    if gain is not None:
        args.append(gain.reshape(1, K).astype(jnp.float32))
        specs.append(pl.BlockSpec((1, K), lambda i: (0, 0)))
    args.append(wb)
    specs.append(pl.BlockSpec((K, n_pad), lambda i: (0, 0)))
    if residual is not None:
        assert n_pad == N
        args.append(residual)
        specs.append(pl.BlockSpec((tm, n_pad), lambda i: (i, 0)))
    out = pl.pallas_call(
        functools.partial(_mm_kernel, has_gain=gain is not None, has_res=residual is not None),
        grid=(M // tm,),
        in_specs=specs,
        out_specs=pl.BlockSpec((tm, n_pad), lambda i: (i, 0)),
        out_shape=jax.ShapeDtypeStruct((M, n_pad), jnp.float32),
        compiler_params=pltpu.CompilerParams(dimension_semantics=("arbitrary",), vmem_limit_bytes=VMEM_LIMIT_BYTES),
        name="rmsnorm_matmul",
    )(*args)
    return out[:, :N] if n_pad != N else out


def _flash_kernel(*refs, R, RK, RM, t, nk, nd, mode):
    it = iter(refs)
    q_ref, k_ref, v_ref = next(it), next(it), next(it)
    nm_ref = e_ref = bias_ref = None
    if RM:
        nm_ref, e_ref = next(it), next(it)
    if mode != 'full':
        bias_ref = next(it)
    o_ref, m_sc, l_sc, acc_sc = next(it), next(it), next(it), next(it)
    qt = pl.program_id(2)
    dk = q_ref.shape[-1]
    q = q_ref[0, 0].reshape(R * t, dk)
    m_sc[...] = jnp.full(m_sc.shape, NEG, jnp.float32)
    l_sc[...] = jnp.zeros(l_sc.shape, jnp.float32)
    acc_sc[...] = jnp.zeros(acc_sc.shape, jnp.float32)
    nt = (((1,), (1,)), ((), ()))

    def step(kt, with_bias):
        ks = pl.ds(pl.multiple_of(kt * t, t), t)
        if RK == 1:
            s = lax.dot_general(q, k_ref[0, 0, 0, ks, :], nt, preferred_element_type=jnp.float32)
        else:
            s = jnp.concatenate(
                [lax.dot_general(q[r * t:(r + 1) * t], k_ref[0, 0, r, ks, :], nt, preferred_element_type=jnp.float32)
                 for r in range(R)], axis=0)
        if RM:
            nb = nm_ref.shape[-1]
            mexp = jnp.dot(nm_ref[0, 0].reshape(RM * t, nb), e_ref[kt], preferred_element_type=jnp.float32)
            if RM == R:
                s = s + mexp
            else:
                s = (s.reshape(R, t, t) + mexp[None]).reshape(R * t, t)
        if with_bias:
            s = s + bias_ref[0, qt - kt]
        m_prev = m_sc[...]
        m_new = jnp.maximum(m_prev, jnp.max(s, axis=-1, keepdims=True))
        alpha = jnp.exp(m_prev - m_new)
        p = jnp.exp(s - m_new)
        l_sc[...] = alpha * l_sc[...] + jnp.sum(p, axis=-1, keepdims=True)
        acc_sc[...] = alpha * acc_sc[...] + jnp.dot(p.astype(jnp.bfloat16), v_ref[0, 0, ks, :],
                                                     preferred_element_type=jnp.float32)
        m_sc[...] = m_new

    def loop(lo, hi, with_bias):
        def body(kt, c):
            step(kt, with_bias)
            return c
        lax.fori_loop(lo, hi, body, 0)

    if mode == 'full':
        for kt in range(nk):
            step(kt, False)
    else:
        near_lo = jnp.maximum(qt - (nd - 1), 0)
        if mode == 'causal':
            loop(0, near_lo, False)
        loop(near_lo, qt + 1, True)
    o_ref[0, 0] = (acc_sc[...] / l_sc[...]).reshape(o_ref.shape[2:])


def _flash(q, k, v, negmask=None, expand=None, bias=None, *, mode, name):
    B, G, R, T, dk = q.shape
    RK, Tk = k.shape[2], k.shape[3]
    dv = v.shape[-1]
    t = min(ATT_TILE, T)
    assert T % t == 0 and Tk % t == 0 and (mode == 'full' or Tk == T)
    RM = 0 if negmask is None else negmask.shape[2]
    nd = 0 if bias is None else bias.shape[1]
    args = [q, k, v]
    specs = [pl.BlockSpec((1, 1, R, t, dk), lambda b, g, i: (b, g, 0, i, 0)),
             pl.BlockSpec((1, 1, RK, Tk, dk), lambda b, g, i: (b, g, 0, 0, 0)),
             pl.BlockSpec((1, 1, Tk, dv), lambda b, g, i: (b, g, 0, 0))]
    if RM:
        nb = negmask.shape[-1]
        args += [negmask, expand]
        specs += [pl.BlockSpec((1, 1, RM, t, nb), lambda b, g, i: (b, g, 0, i, 0)),
                  pl.BlockSpec(expand.shape, lambda b, g, i: (0, 0, 0))]
    if mode != 'full':
        args.append(bias)
        specs.append(pl.BlockSpec((1, nd, R * t, t), lambda b, g, i: (g, 0, 0, 0)))
    return pl.pallas_call(
        functools.partial(_flash_kernel, R=R, RK=RK, RM=RM, t=t, nk=Tk // t, nd=nd, mode=mode),
        grid=(B, G, T // t),
        in_specs=specs,
        out_specs=pl.BlockSpec((1, 1, R, t, dv), lambda b, g, i: (b, g, 0, i, 0)),
        out_shape=jax.ShapeDtypeStruct((B, G, R, T, dv), jnp.float32),
        scratch_shapes=[pltpu.VMEM((R * t, 1), jnp.float32), pltpu.VMEM((R * t, 1), jnp.float32),
                        pltpu.VMEM((R * t, dv), jnp.float32)],
        compiler_params=pltpu.CompilerParams(dimension_semantics=("arbitrary", "arbitrary", "arbitrary"),
                                             vmem_limit_bytes=VMEM_LIMIT_BYTES),
        name=name,
    )(*args)


def _page_specs(n_per_step, n_pages, li, block):
    def index_map(b, c, pt_ref, *, j):
        return (li, pt_ref[b * n_pages + c * n_per_step + j]) + (0,) * (len(block) - 2)
    return [pl.BlockSpec(block, functools.partial(index_map, j=j)) for j in range(n_per_step)]


def _pages_per_step(n_pages):
    pp = 16 if n_pages % 16 == 0 else 8
    assert n_pages % pp == 0
    return pp


def _softmax_update(s, v, m_sc, l_sc, acc_sc):
    m_prev = m_sc[...]
    m_new = jnp.maximum(m_prev, jnp.max(s, axis=-1, keepdims=True))
    alpha = jnp.exp(m_prev - m_new)
    p = jnp.exp(s - m_new)
    l_sc[...] = alpha * l_sc[...] + jnp.sum(p, axis=-1, keepdims=True)
    acc_sc[...] = alpha * acc_sc[...] + jnp.dot(p.astype(jnp.bfloat16), v, preferred_element_type=jnp.float32)
    m_sc[...] = m_new


_NT = (((1,), (1,)), ((), ()))


def _paged_attn_kernel(pt_ref, q_ref, *refs, pp, nch, nbc):
    k_refs, v_refs = refs[:pp], refs[pp:2 * pp]
    nm_ref, e_ref, bias_ref, knew_ref, vnew_ref, bnew_ref, o_ref, m_sc, l_sc, acc_sc = refs[2 * pp:]
    c = pl.program_id(1)

    @pl.when(c == 0)
    def _():
        m_sc[...] = jnp.full(m_sc.shape, NEG, jnp.float32)
        l_sc[...] = jnp.zeros(l_sc.shape, jnp.float32)
        acc_sc[...] = jnp.zeros(acc_sc.shape, jnp.float32)

    q = q_ref[0]
    k = jnp.concatenate([r[0, 0] for r in k_refs], axis=0).astype(jnp.bfloat16)
    v = jnp.concatenate([r[0, 0] for r in v_refs], axis=0).astype(jnp.bfloat16)
    s = lax.dot_general(q, k, _NT, preferred_element_type=jnp.float32)
    s = s + jnp.dot(nm_ref[0], e_ref[c], preferred_element_type=jnp.float32)
    near = (c >= nch - nbc).astype(jnp.float32)
    s = s + near * bias_ref[0]
    _softmax_update(s, v, m_sc, l_sc, acc_sc)

    @pl.when(c == nch - 1)
    def _():
        s_new = lax.dot_general(q, knew_ref[0], _NT, preferred_element_type=jnp.float32) + bnew_ref[...]
        _softmax_update(s_new, vnew_ref[0], m_sc, l_sc, acc_sc)
        o_ref[0] = acc_sc[...] / l_sc[...]


def _paged_attn(q_bd, pool_k, pool_v, li, pt_flat, n_pages, negmask, expand, bias_last, k_new, v_new, bias_new, *, name):
    B, RQ, W = q_bd.shape
    pp = _pages_per_step(n_pages)
    nch = n_pages // pp
    nbc = bias_last.shape[0]
    nb = negmask.shape[-1]
    page = (1, 1, PAGE_SIZE, W)
    in_specs = ([pl.BlockSpec((1, RQ, W), lambda b, c, pt: (b, 0, 0))]
                + _page_specs(pp, n_pages, li, page) + _page_specs(pp, n_pages, li, page)
                + [pl.BlockSpec((1, RQ, nb), lambda b, c, pt: (b, 0, 0)),
                   pl.BlockSpec(expand.shape, lambda b, c, pt: (0, 0, 0)),
                   pl.BlockSpec((1, RQ, pp * PAGE_SIZE), lambda b, c, pt: (jnp.maximum(c - (nch - nbc), 0), 0, 0)),
                   pl.BlockSpec((1, PAGE_SIZE, W), lambda b, c, pt: (b, 0, 0)),
                   pl.BlockSpec((1, PAGE_SIZE, W), lambda b, c, pt: (b, 0, 0)),
                   pl.BlockSpec((RQ, PAGE_SIZE), lambda b, c, pt: (0, 0))])
    return pl.pallas_call(
        functools.partial(_paged_attn_kernel, pp=pp, nch=nch, nbc=nbc),
        grid_spec=pltpu.PrefetchScalarGridSpec(
            num_scalar_prefetch=1, grid=(B, nch), in_specs=in_specs,
            out_specs=pl.BlockSpec((1, RQ, W), lambda b, c, pt: (b, 0, 0)),
            scratch_shapes=[pltpu.VMEM((RQ, 1), jnp.float32), pltpu.VMEM((RQ, 1), jnp.float32),
                            pltpu.VMEM((RQ, W), jnp.float32)]),
        out_shape=jax.ShapeDtypeStruct((B, RQ, W), jnp.float32),
        compiler_params=pltpu.CompilerParams(dimension_semantics=("arbitrary", "arbitrary"),
                                             vmem_limit_bytes=VMEM_LIMIT_BYTES),
        name=name,
    )(pt_flat, q_bd, *([pool_k] * pp), *([pool_v] * pp), negmask, expand, bias_last, k_new, v_new, bias_new)


def _paged_mla_kernel(pt_ref, qn_ref, qr_ref, *refs, pp, nch, n_heads, n_q):
    c_refs, r_refs = refs[:pp], refs[pp:2 * pp]
    (wuk_ref, grope_ref, hsel_ref, cos_ref, sin_ref, cnew_ref, rnew_ref, cosn_ref, sinn_ref, bnew_ref,
     o_ref, m_sc, l_sc, acc_sc) = refs[2 * pp:]
    c = pl.program_id(1)

    @pl.when(c == 0)
    def _():
        m_sc[...] = jnp.full(m_sc.shape, NEG, jnp.float32)
        l_sc[...] = jnp.zeros(l_sc.shape, jnp.float32)
        acc_sc[...] = jnp.zeros(acc_sc.shape, jnp.float32)

    def head_sums(x, sel):
        hi = x.astype(jnp.bfloat16)
        lo = (x - hi.astype(jnp.float32)).astype(jnp.bfloat16)
        return (lax.dot_general(sel, hi, _NT, preferred_element_type=jnp.float32)
                + lax.dot_general(sel, lo, _NT, preferred_element_type=jnp.float32))

    def scores(c32, kr32, cos, sin):
        cb = c32.astype(jnp.bfloat16)
        kn = jnp.dot(cb, wuk_ref[...], preferred_element_type=jnp.float32)
        ss = head_sums(kn * kn, hsel_ref[...]) + head_sums(kr32 * kr32, jnp.ones((8, ROPE_D), jnp.bfloat16))
        inv = lax.rsqrt(ss / QK_D + EPS) * QK_D ** -0.5
        x = kr32 * grope_ref[...]
        half = ROPE_D // 2
        x1, x2 = x[:, :half], x[:, half:]
        krr = jnp.concatenate([x1 * cos - x2 * sin, x2 * cos + x1 * sin], axis=1).astype(jnp.bfloat16)
        s = (lax.dot_general(qn_ref[0], kn.astype(jnp.bfloat16), _NT, preferred_element_type=jnp.float32)
             + lax.dot_general(qr_ref[0], krr, _NT, preferred_element_type=jnp.float32))
        n = s.shape[1]
        inv_rows = jnp.concatenate([jnp.broadcast_to(inv[h:h + 1], (n_q, n)) for h in range(n_heads)], axis=0)
        return s * inv_rows, cb

    c32 = jnp.concatenate([r[0, 0] for r in c_refs], axis=0)
    kr32 = jnp.concatenate([r[0, 0] for r in r_refs], axis=0)
    s, cb = scores(c32, kr32, cos_ref[0], sin_ref[0])
    _softmax_update(s, cb, m_sc, l_sc, acc_sc)

    @pl.when(c == nch - 1)
    def _():
        s_new, cb_new = scores(cnew_ref[0], rnew_ref[0], cosn_ref[...], sinn_ref[...])
        _softmax_update(s_new + bnew_ref[...], cb_new, m_sc, l_sc, acc_sc)
        o_ref[0] = acc_sc[...] / l_sc[...]


def _paged_mla(qn_bd, qr, pool_c, pool_r, li, pt_flat, n_pages, wuk, grope, cos, sin, c_new, r_new, cos_new, sin_new,
               bias_new, n_heads):
    B, RQ, _ = qn_bd.shape
    pp = _pages_per_step(n_pages)
    nch = n_pages // pp
    half = ROPE_D // 2
    hsel = (jnp.arange(8)[:, None] == (jnp.arange(n_heads * NOPE_D) // NOPE_D)[None, :]).astype(jnp.bfloat16)
    const = lambda shape: pl.BlockSpec(shape, lambda b, c, pt: (0,) * len(shape))
    per_b = lambda shape: pl.BlockSpec(shape, lambda b, c, pt: (b,) + (0,) * (len(shape) - 1))
    in_specs = ([per_b((1, RQ, n_heads * NOPE_D)), per_b((1, RQ, ROPE_D))]
                + _page_specs(pp, n_pages, li, (1, 1, PAGE_SIZE, KV_LORA))
                + _page_specs(pp, n_pages, li, (1, 1, PAGE_SIZE, ROPE_D))
                + [const(wuk.shape), const((1, ROPE_D)), const(hsel.shape),
                   pl.BlockSpec((1, pp * PAGE_SIZE, half), lambda b, c, pt: (c, 0, 0)),
                   pl.BlockSpec((1, pp * PAGE_SIZE, half), lambda b, c, pt: (c, 0, 0)),
                   per_b((1, PAGE_SIZE, KV_LORA)), per_b((1, PAGE_SIZE, ROPE_D)),
                   const((PAGE_SIZE, half)), const((PAGE_SIZE, half)), const((RQ, PAGE_SIZE))])
    return pl.pallas_call(
        functools.partial(_paged_mla_kernel, pp=pp, nch=nch, n_heads=n_heads, n_q=RQ // n_heads),
        grid_spec=pltpu.PrefetchScalarGridSpec(
            num_scalar_prefetch=1, grid=(B, nch), in_specs=in_specs,
            out_specs=per_b((1, RQ, KV_LORA)),
            scratch_shapes=[pltpu.VMEM((RQ, 1), jnp.float32), pltpu.VMEM((RQ, 1), jnp.float32),
                            pltpu.VMEM((RQ, KV_LORA), jnp.float32)]),
        out_shape=jax.ShapeDtypeStruct((B, RQ, KV_LORA), jnp.float32),
        compiler_params=pltpu.CompilerParams(dimension_semantics=("arbitrary", "arbitrary"),
                                             vmem_limit_bytes=VMEM_LIMIT_BYTES),
        name="mla_sample",
    )(pt_flat, qn_bd, qr, *([pool_c] * pp), *([pool_r] * pp), wuk, grope, hsel, cos, sin, c_new, r_new,
      cos_new, sin_new, bias_new)


def _paged_cmp_proj_kernel(pt_ref, *refs, pp):
    x = jnp.concatenate([r[0, 0] for r in refs[:pp]], axis=0).astype(jnp.bfloat16)
    w_ref, o_ref = refs[pp:]
    o_ref[0] = jnp.dot(x, w_ref[...], preferred_element_type=jnp.float32)


def _paged_cmp_proj(pool, li, pt_flat, B, n_pages, wflat):
    pp = _pages_per_step(n_pages)
    cpp, width = pool.shape[2], pool.shape[3]
    n_out = wflat.shape[1]
    return pl.pallas_call(
        functools.partial(_paged_cmp_proj_kernel, pp=pp),
        grid_spec=pltpu.PrefetchScalarGridSpec(
            num_scalar_prefetch=1, grid=(B, n_pages // pp),
            in_specs=_page_specs(pp, n_pages, li, (1, 1, cpp, width))
            + [pl.BlockSpec(wflat.shape, lambda b, c, pt: (0, 0))],
            out_specs=pl.BlockSpec((1, pp * cpp, n_out), lambda b, c, pt: (b, c, 0))),
        out_shape=jax.ShapeDtypeStruct((B, n_pages * cpp, n_out), jnp.float32),
        compiler_params=pltpu.CompilerParams(dimension_semantics=("arbitrary", "arbitrary"),
                                             vmem_limit_bytes=VMEM_LIMIT_BYTES),
        name="nsa_compress_sample",
    )(pt_flat, *([pool] * pp), wflat)


def _paged_sum_kernel(pt_ref, *refs, pp):
    o_ref = refs[pp]
    o_ref[0] = jnp.concatenate([jnp.sum(r[0, 0], axis=0, keepdims=True) for r in refs[:pp]], axis=0)


def _paged_sum(pool, li, pt_flat, B, n_pages):
    pp = _pages_per_step(n_pages)
    W = pool.shape[-1]
    return pl.pallas_call(
        functools.partial(_paged_sum_kernel, pp=pp),
        grid_spec=pltpu.PrefetchScalarGridSpec(
            num_scalar_prefetch=1, grid=(B, n_pages // pp),
            in_specs=_page_specs(pp, n_pages, li, (1, 1, PAGE_SIZE, W)),
            out_specs=pl.BlockSpec((1, pp, W), lambda b, c, pt: (b, c, 0))),
        out_shape=jax.ShapeDtypeStruct((B, n_pages, W), jnp.float32),
        compiler_params=pltpu.CompilerParams(dimension_semantics=("arbitrary", "arbitrary"),
                                             vmem_limit_bytes=VMEM_LIMIT_BYTES),
        name="moba_page_sums",
    )(pt_flat, *([pool] * pp))


def rms_norm(x, g):
    xf = x.astype(jnp.float32)
    y = xf * lax.rsqrt(jnp.mean(xf * xf, axis=-1, keepdims=True) + EPS)
    return (y * g.astype(jnp.float32)).astype(x.dtype)


def split_cols(h, sizes):
    return jnp.split(h, np.cumsum(sizes)[:-1].tolist(), axis=-1)


def masked_softmax(s, mask):
    p = jax.nn.softmax(jnp.where(mask, s, NEG), axis=-1)
    return p * mask


def t5_bucket(dist):
    dist = jnp.maximum(dist, 0)
    exact = N_BUCKETS // 2
    far = exact + (jnp.log(jnp.maximum(dist, 1).astype(jnp.float32) / exact)
                   / math.log(T5_MAX_DIST / exact) * (N_BUCKETS - exact)).astype(jnp.int32)
    return jnp.where(dist < exact, dist, jnp.minimum(far, N_BUCKETS - 1))


def rope(x, pos):
    half = ROPE_D // 2
    inv = ROPE_THETA ** (-jnp.arange(half, dtype=jnp.float32) / half)
    ang = pos.astype(jnp.float32)[:, None] * inv
    ang = ang.reshape((ang.shape[0],) + (1,) * (x.ndim - 3) + (half,))
    cos, sin = jnp.cos(ang), jnp.sin(ang)
    xf = x.astype(jnp.float32)
    x1, x2 = xf[..., :half], xf[..., half:]
    return jnp.concatenate([x1 * cos - x2 * sin, x2 * cos + x1 * sin], axis=-1).astype(x.dtype)


def _bias_lookup(rel_bias, bucket):
    onehot = (bucket[..., None] == jnp.arange(N_BUCKETS)).astype(jnp.float32)
    return jnp.einsum('...k,kh->...h', onehot, rel_bias.astype(jnp.float32), precision=lax.Precision.HIGHEST)


def _sample_bias(rel_bias, S, past, keys_per_chunk, nbc):
    H = rel_bias.shape[1]
    rel = rel_bias - rel_bias[N_BUCKETS - 1]
    n = nbc * keys_per_chunk
    s = jnp.arange(S)
    dist = (past + s)[:, None] - (past - n + jnp.arange(n))[None, :]
    b = jnp.transpose(_bias_lookup(rel, t5_bucket(dist)), (2, 0, 1)).reshape(H * S, nbc, keys_per_chunk)
    j = jnp.arange(PAGE_SIZE)
    dn = s[:, None] - j[None, :]
    ok = (dn >= 0) & (j < S)[None, :]
    bn = jnp.where(ok[..., None], _bias_lookup(rel, t5_bucket(dn)), NEG)
    return jnp.transpose(b, (1, 0, 2)), jnp.transpose(bn, (2, 0, 1)).reshape(H * S, PAGE_SIZE)


def _block_diag_rows(q, groups):
    B, S, H, D = q.shape
    qt = jnp.transpose(q, (0, 2, 1, 3)).reshape(B, groups, (H // groups) * S, D)
    out = qt[:, :, :, None, :] * jnp.eye(groups, dtype=q.dtype)[None, :, None, :, None]
    return out.reshape(B, H * S, groups * D)


def _block_diag_extract(o, groups, S):
    B, RQ, W = o.shape
    D = W // groups
    o5 = o.reshape(B, groups, RQ // groups, groups, D)
    od = jnp.stack([o5[:, g, :, g, :] for g in range(groups)], axis=1)
    return jnp.transpose(od.reshape(B, RQ // S, S, D), (0, 2, 1, 3))


def _bias_tiles(rel_bias, heads_per_group, t, nd, window=None):
    i = jnp.arange(t)[:, None]
    j = jnp.arange(t)[None, :]
    dist = jnp.arange(nd)[:, None, None] * t + i - j
    b = _bias_lookup(rel_bias - rel_bias[N_BUCKETS - 1], t5_bucket(dist))
    ok = dist >= 0
    if window is not None:
        ok = ok & (dist < window)
    b = jnp.where(ok[..., None], b, NEG)
    n_heads = rel_bias.shape[1]
    b = jnp.transpose(b, (3, 0, 1, 2)).reshape(n_heads // heads_per_group, heads_per_group, nd, t, t)
    return jnp.transpose(b, (0, 2, 1, 3, 4)).reshape(n_heads // heads_per_group, nd, heads_per_group * t, t).astype(jnp.float32)


def _causal_tiles(n_groups, rows, t):
    ok = jnp.arange(t)[:, None] >= jnp.arange(t)[None, :]
    b = jnp.where(ok, 0.0, NEG).astype(jnp.float32)
    return jnp.broadcast_to(jnp.tile(b, (rows, 1))[None, None], (n_groups, 1, rows * t, t))


def _expand_onehot(n_keys, blk, nb, t):
    key_blk = (jnp.arange(n_keys) // blk).reshape(n_keys // t, 1, t)
    return (key_blk == jnp.arange(nb)[None, :, None]).astype(jnp.bfloat16)


def _heads_first(x, groups):
    B, T, H, D = x.shape
    return jnp.transpose(x.reshape(B, T, groups, H // groups, D), (0, 2, 3, 1, 4))


def _heads_last(o):
    B, G, R, T, D = o.shape
    return jnp.transpose(o, (0, 3, 1, 2, 4)).reshape(B, T, G * R, D)


def cmp_chunk_proj(raw, w1):
    B, L = raw.shape[:2]
    ch = raw.reshape(B, L // CMP_STRIDE, CMP_STRIDE, NSA_G, HD)
    a = jnp.einsum('bnsgd,sdh->bngh', ch, w1[:CMP_STRIDE])
    b = jnp.einsum('bnsgd,sdh->bngh', ch, w1[CMP_STRIDE:])
    return a, b


def cmp_finish(a, b, pe, w1, w2):
    pe_h = jnp.einsum('sd,sdh->h', pe, w1)
    h = jax.nn.gelu(a[:, :-1] + b[:, 1:] + pe_h)
    return jnp.einsum('bngh,hd->bngd', h, w2)


def nsa_compress(raw_k_parts, raw_v_parts, W):
    out = []
    for j, parts in enumerate((raw_k_parts, raw_v_parts)):
        w1 = W['cmp_w1'][j]
        ab = [cmp_chunk_proj(r, w1) for r in parts]
        a = jnp.concatenate([t[0] for t in ab], axis=1)
        b = jnp.concatenate([t[1] for t in ab], axis=1)
        out.append(cmp_finish(a, b, W['cmp_pe'][j], w1, W['cmp_w2'][j]))
    return rms_norm(out[0], W['nsa_kn'][0]), out[1]


def sel_block_scores(imp, nsb):
    r = SEL_BLK // CMP_STRIDE
    front = CMP_LEN // CMP_STRIDE - 1
    nc = imp.shape[-1]
    imp = jnp.pad(imp, [(0, 0)] * (imp.ndim - 1) + [(front, r * nsb - nc)])
    score = 0.0
    for o in range(-front, r):
        lo = o * CMP_STRIDE
        w = max(0, min(lo + CMP_LEN, SEL_BLK) - max(lo, 0)) / CMP_LEN
        start = o + front
        score = score + w * imp[..., start:start + r * (nsb - 1) + 1:r]
    return score


def _nsa_compressed_and_select(q, q_pos, kc, vc, nsb, bias_t):
    B, Q = q.shape[:2]
    scale = HD ** -0.5
    f32 = jnp.float32
    qg = q.reshape(B, Q, NSA_G, NSA_HPG, HD)
    nc = kc.shape[1]
    dist_c = q_pos[:, None] - (jnp.arange(nc) * CMP_STRIDE + CMP_LEN - 1)[None, :]
    bias_c = _bias_lookup(bias_t, t5_bucket(dist_c)).reshape(Q, nc, NSA_G, NSA_HPG)
    bias_c = jnp.transpose(bias_c, (0, 2, 3, 1))[None]
    s_c =jnp.einsum('bqgpd,bngd->bqgpn', qg, kc, preferred_element_type=f32) * scale + bias_c
    p_c = masked_softmax(s_c, (dist_c >= 0)[None, :, None, None, :])
    o_c = jnp.einsum('bqgpn,bngd->bqgpd', p_c.astype(vc.dtype), vc)
    score = sel_block_scores(p_c.sum(axis=3), nsb)
    blk = jnp.arange(nsb)[None, :]
    own = (q_pos // SEL_BLK)[:, None]
    forced = (blk == 0) | (blk == own) | (blk == own - 1)
    score = jnp.where(forced[None, :, None, :], FORCE,
                      jnp.where((blk <= own)[None, :, None, :], score, -jnp.inf))
    k_eff = min(SEL_TOPK, nsb)
    _, idx = lax.top_k(score, k_eff)
    return o_c, idx


def _nsa_window_sample(q, q_pos, kw, vw, kw_pos, bias_t):
    B, Q = q.shape[:2]
    qg = q.reshape(B, Q, NSA_G, NSA_HPG, HD)
    dist_w = q_pos[:, None] - kw_pos[None, :]
    mask_w = (dist_w >= 0) & (dist_w < NSA_WINDOW) & (kw_pos >= 0)[None, :]
    bias_w = _bias_lookup(bias_t, t5_bucket(dist_w)).reshape(Q, kw_pos.shape[0], NSA_G, NSA_HPG)
    bias_w = jnp.transpose(bias_w, (0, 2, 3, 1))[None]
    s_w = jnp.einsum('bqgpd,blgd->bqgpl', qg, kw, preferred_element_type=jnp.float32) * HD ** -0.5 + bias_w
    p_w = masked_softmax(s_w, mask_w[None, :, None, None, :])
    return jnp.einsum('bqgpl,blgd->bqgpd', p_w.astype(vw.dtype), vw)


def _gqa_sample_attention(q, sel_rows, pool_k, pool_v, li, pt_flat, n_pages, blk, k_new, v_new, bias_t, groups, name):
    B, S, H, D = q.shape
    L, n_pool = pool_k.shape[:2]
    past = n_pages * PAGE_SIZE
    assert past % blk == 0 and S <= blk and S <= PAGE_SIZE
    pp = _pages_per_step(n_pages)
    keys = pp * PAGE_SIZE
    nbc = min(n_pages // pp, -(-T5_SATURATED_DIST // keys))
    W = groups * D
    q_bd = _block_diag_rows(q * D ** -0.5, groups).astype(jnp.bfloat16)
    negmask = jnp.where(sel_rows, 0.0, NEG).astype(jnp.bfloat16)
    expand = _expand_onehot(past, blk, past // blk, keys)
    bias_last, bias_new = _sample_bias(bias_t, S, past, keys, nbc)
    padr = ((0, 0), (0, PAGE_SIZE - S), (0, 0))
    kn = jnp.pad(k_new.reshape(B, S, W), padr).astype(jnp.bfloat16)
    vn = jnp.pad(v_new.reshape(B, S, W), padr).astype(jnp.bfloat16)
    o = _paged_attn(q_bd, pool_k.reshape(L, n_pool, PAGE_SIZE, W), pool_v.reshape(L, n_pool, PAGE_SIZE, W), li, pt_flat,
                    n_pages, negmask, expand, bias_last, kn, vn, bias_new, name=name)
    return _block_diag_extract(o, groups, S)


def _nsa_compress_sample(pools, li, pt_flat, B, n_pages, new_parts, W):
    out = []
    chunks_per_page = PAGE_SIZE // CMP_STRIDE
    for j, (pool, new) in enumerate(zip(pools, new_parts)):
        w1 = W['cmp_w1'][j]
        L, n_pool = pool.shape[:2]
        w1r = w1.reshape(2, CMP_STRIDE, HD, CMP_HID)
        wflat = jnp.einsum('wsdh,gk->sgdwkh', w1r, jnp.eye(NSA_G, dtype=w1.dtype))
        wflat = wflat.reshape(CMP_STRIDE * NSA_G * HD, 2 * NSA_G * CMP_HID).astype(jnp.bfloat16)
        ab = _paged_cmp_proj(pool.reshape(L, n_pool, chunks_per_page, CMP_STRIDE * NSA_G * HD), li, pt_flat, B, n_pages, wflat)
        ab = ab.reshape(B, n_pages * chunks_per_page, 2, NSA_G, CMP_HID)
        a_new, b_new = cmp_chunk_proj(new, w1)
        a = jnp.concatenate([ab[:, :, 0], a_new], axis=1)
        b = jnp.concatenate([ab[:, :, 1], b_new], axis=1)
        out.append(cmp_finish(a, b, W['cmp_pe'][j], w1, W['cmp_w2'][j]))
    return rms_norm(out[0], W['nsa_kn'][0]), out[1]


def mla_keys(ckv, kr, k_pos, W):
    kn = jnp.einsum('blc,chd->blhd', ckv, W['mla_wuk'])
    krf = kr.astype(jnp.float32)
    ss = jnp.einsum('blhd,blhd->blh', kn, kn, preferred_element_type=jnp.float32) + jnp.sum(krf * krf, -1)[..., None]
    inv = lax.rsqrt(ss / QK_D + EPS)
    krr = rope(kr * W['mla_kn'][NOPE_D:], k_pos)
    return kn, inv, krr


def _rope_tables(pos):
    half = ROPE_D // 2
    inv = ROPE_THETA ** (-jnp.arange(half, dtype=jnp.float32) / half)
    ang = pos.astype(jnp.float32)[:, None] * inv
    return jnp.cos(ang), jnp.sin(ang)


def _mla_sample(P, c_ckv, c_kr, li, pt_flat, n_pages, W):
    B, S = P['ckv'].shape[:2]
    assert S <= PAGE_SIZE
    past = n_pages * PAGE_SIZE
    pp = _pages_per_step(n_pages)
    qn = _block_diag_rows(P['q_nope'] * W['mla_kn'][:NOPE_D], MLA_H).astype(jnp.bfloat16)
    qr = jnp.transpose(P['q_rope'], (0, 2, 1, 3)).reshape(B, MLA_H * S, ROPE_D).astype(jnp.bfloat16)
    cos, sin = _rope_tables(jnp.arange(past))
    cos = cos.reshape(n_pages // pp, pp * PAGE_SIZE, -1)
    sin = sin.reshape(n_pages // pp, pp * PAGE_SIZE, -1)
    cos_new, sin_new = _rope_tables(past + jnp.arange(PAGE_SIZE))
    padr = ((0, 0), (0, PAGE_SIZE - S), (0, 0))
    j = jnp.arange(PAGE_SIZE)[None, :]
    s = jnp.arange(S)[:, None]
    bias_new = jnp.tile(jnp.where((j <= s) & (j < S), 0.0, NEG).astype(jnp.float32), (MLA_H, 1))
    o_lat = _paged_mla(qn, qr, c_ckv, c_kr, li, pt_flat, n_pages,
                       W['mla_wuk'].reshape(KV_LORA, MLA_H * NOPE_D).astype(jnp.bfloat16),
                       W['mla_kn'][NOPE_D:].reshape(1, ROPE_D).astype(jnp.float32), cos, sin,
                       jnp.pad(P['ckv'], padr), jnp.pad(P['kr'], padr), cos_new, sin_new, bias_new, MLA_H)
    o_lat = jnp.transpose(o_lat.reshape(B, MLA_H, S, KV_LORA), (0, 2, 1, 3))
    return jnp.einsum('bqhc,chd->bqhd', o_lat, W['mla_wuv']).reshape(B, S, MLA_H * V_D)


def _mla_prompt(P, pos, W):
    kn, inv, krr = mla_keys(P['ckv'], P['kr'], pos, W)
    B, T = kn.shape[:2]
    kfull = jnp.concatenate([kn, jnp.broadcast_to(krr[:, :, None, :], (B, T, MLA_H, ROPE_D))], axis=-1)
    kfull = kfull * (inv * QK_D ** -0.5)[..., None]
    qfull = jnp.concatenate([P['q_nope'] * W['mla_kn'][:NOPE_D], P['q_rope']], axis=-1)
    padd = ((0, 0), (0, 0), (0, 0), (0, LANES - QK_D))
    kk = _heads_first(jnp.pad(kfull, padd), 1).astype(jnp.bfloat16)
    qq = _heads_first(jnp.pad(qfull, padd), 1).astype(jnp.bfloat16)
    vv = P['ckv'].astype(jnp.bfloat16)[:, None]
    t = min(ATT_TILE, T)
    o_lat = _flash(qq, kk, vv, bias=_causal_tiles(1, MLA_H, t), mode='causal', name="mla_prompt")
    o_lat = _heads_last(o_lat)
    return jnp.einsum('bqhc,chd->bqhd', o_lat, W['mla_wuv']).reshape(B, T, MLA_H * V_D)


def moba_means_paged(page_sum, n_pages, k_new):
    B = page_sum.shape[0]
    past = n_pages * PAGE_SIZE
    total = past + k_new.shape[1]
    nbm = -(-total // MOBA_BLK)
    pps = MOBA_BLK // PAGE_SIZE
    npp = -(-n_pages // pps) * pps
    page_sum = jnp.pad(page_sum, ((0, 0), (0, npp - n_pages), (0, 0), (0, 0)))
    blk_sum = page_sum.reshape(B, npp // pps, pps, MOBA_KVH, HD).sum(axis=2)
    blk_sum = jnp.pad(blk_sum, ((0, 0), (0, nbm - npp // pps), (0, 0), (0, 0)))
    new_blk = (past + jnp.arange(k_new.shape[1])) // MOBA_BLK
    onehot = (new_blk[:, None] == jnp.arange(nbm)[None, :]).astype(jnp.float32)
    blk_sum = blk_sum + jnp.einsum('bsgd,sn->bngd', k_new.astype(jnp.float32), onehot)
    return blk_sum / MOBA_BLK


def _moba_pick_blocks(q, q_pos, kmean_h):
    nbm = kmean_h.shape[1]
    own = q_pos // MOBA_BLK
    blk = jnp.arange(nbm)
    full_past = (blk[None, :] < own[:, None])[None, :, None, :]
    k_eff = min(MOBA_TOPK, nbm - 1)
    if k_eff <= 0:
        return jnp.zeros(q.shape[:3] + (nbm,), bool)
    gs = jnp.einsum('bqhd,bnhd->bqhn', q, kmean_h, preferred_element_type=jnp.float32)
    gs = jnp.where(full_past, gs, -jnp.inf)
    _, idx = lax.top_k(gs, k_eff)
    return jnp.any(idx[..., None] == blk, axis=-2) & full_past


def mem_kv(mem, g, wkv, kn):
    B, N = mem.shape[:2]
    h = _matmul(mem.reshape(B * N, -1), wkv, gain=g).reshape(B, N, 2, MEM_H, HD)
    return rms_norm(h[:, :, 0], kn), h[:, :, 1]


def mem_attend(q, km, vm):
    B, Q = q.shape[:2]
    s = jnp.einsum('bqhd,bmhd->bhqm', q, km, preferred_element_type=jnp.float32) * HD ** -0.5
    p = jax.nn.softmax(s, axis=-1).astype(vm.dtype)
    return jnp.einsum('bhqm,bmhd->bqhd', p, vm).reshape(B, Q, MEM_H * HD)


def _mem_attend_prompt(q, km, vm):
    B, T = q.shape[:2]
    qq = _heads_first(q * HD ** -0.5, MEM_H).astype(jnp.bfloat16)
    kk = _heads_first(km, MEM_H).astype(jnp.bfloat16)
    vv = _heads_first(vm, MEM_H)[:, :, 0].astype(jnp.bfloat16)
    return _heads_last(_flash(qq, kk, vv, mode='full', name="mem_prompt")).reshape(B, T, MEM_H * HD)


def even_project(x, ln, W, pos):
    B, T = x.shape[:2]
    h = _matmul(x.reshape(B * T, -1), W['w_in'], gain=ln).reshape(B, T, -1)
    nq, nkv, ngt, nz, cq, ckv, kr, mz, mq, memz = split_cols(h, EVEN_SIZES)
    kv = nkv.reshape(B, T, 6, NSA_G, HD)
    qm = _matmul(cq.reshape(B * T, -1), W['mla_wuq'], gain=W['mla_cqn']).reshape(B, T, MLA_H, QK_D)
    qm = rms_norm(qm, W['mla_qn'])
    return {
        'q': rms_norm(nq.reshape(B, T, NSA_H, HD), W['nsa_qn']),
        'gates': jax.nn.sigmoid(ngt.reshape(B, T, NSA_H, 3)),
        'raw_kc': kv[:, :, 0], 'raw_vc': kv[:, :, 1],
        'k_sel': rms_norm(kv[:, :, 2], W['nsa_kn'][1]), 'v_sel': kv[:, :, 3],
        'k_win': rms_norm(kv[:, :, 4], W['nsa_kn'][2]), 'v_win': kv[:, :, 5],
        'q_nope': qm[..., :NOPE_D], 'q_rope': rope(qm[..., NOPE_D:], pos),
        'ckv': rms_norm(ckv, W['mla_ckvn']), 'kr': kr,
        'qmem': rms_norm(mq.reshape(B, T, MEM_H, HD), W['mem_qn']),
        'z': (nz, mz, memz),
    }


def even_finish(x, P, o_nsa, o_mla, o_mem, W):
    z_nsa, z_mla, z_mem = P['z']
    mixed = jnp.concatenate([o_nsa * jax.nn.silu(z_nsa), o_mla * jax.nn.silu(z_mla), o_mem * jax.nn.silu(z_mem)], axis=-1)
    B, T, D = x.shape
    return _matmul(mixed.reshape(B * T, -1), W['w_out'], residual=x.reshape(B * T, D)).reshape(B, T, D)


def _nsa_prompt(P, kc, vc, bias_t):
    q = P['q']
    B, T = q.shape[:2]
    nsb = T // SEL_BLK
    t = min(ATT_TILE, T)
    o_c, idx = _nsa_compressed_and_select(q, jnp.arange(T), kc, vc, nsb, bias_t)
    sel = jnp.any(idx[..., None] == jnp.arange(nsb), axis=-2)
    sel = sel & (jnp.arange(nsb)[None, :] <= (jnp.arange(T) // SEL_BLK)[:, None])[None, :, None, :]
    negmask = jnp.where(sel, 0.0, NEG).astype(jnp.bfloat16)
    negmask = jnp.transpose(negmask, (0, 2, 1, 3))[:, :, None]
    qq = _heads_first(q * HD ** -0.5, NSA_G).astype(jnp.bfloat16)
    o_s = _flash(qq, _heads_first(P['k_sel'], NSA_G).astype(jnp.bfloat16),
                 _heads_first(P['v_sel'], NSA_G)[:, :, 0].astype(jnp.bfloat16),
                 negmask, _expand_onehot(T, SEL_BLK, nsb, t),
                 _bias_tiles(bias_t, NSA_HPG, t, min(T5_NEAR_TILES, T // t)), mode='causal', name="nsa_selected_prompt")
    n_win = -(-(NSA_WINDOW - 1) // t) + 1
    o_w = _flash(qq, _heads_first(P['k_win'], NSA_G).astype(jnp.bfloat16),
                 _heads_first(P['v_win'], NSA_G)[:, :, 0].astype(jnp.bfloat16),
                 bias=_bias_tiles(bias_t, NSA_HPG, t, min(n_win, T // t), window=NSA_WINDOW),
                 mode='window', name="nsa_window_prompt")
    g = P['gates'].reshape(B, T, NSA_G, NSA_HPG, 3, 1)
    o_s = _heads_last(o_s).reshape(B, T, NSA_G, NSA_HPG, HD)
    o_w = _heads_last(o_w).reshape(B, T, NSA_G, NSA_HPG, HD)
    o = g[..., 0, :] * o_c + g[..., 1, :] * o_s + g[..., 2, :] * o_w
    return o.reshape(B, T, NSA_H * HD)


def even_prompt(x, ln, W, km, vm, bias_t):
    B, T = x.shape[:2]
    pos = jnp.arange(T, dtype=jnp.int32)
    P = even_project(x, ln, W, pos)
    kc, vc = nsa_compress([P['raw_kc']], [P['raw_vc']], W)
    o_nsa = _nsa_prompt(P, kc, vc, bias_t)
    o_mla = _mla_prompt(P, pos, W)
    o_mem = _mem_attend_prompt(P['qmem'], km, vm)
    y = even_finish(x, P, o_nsa, o_mla, o_mem, W)
    wk = min(NSA_WINDOW, T)
    state = (P['raw_kc'], P['raw_vc'], P['k_sel'], P['v_sel'], P['k_win'][:, T - wk:], P['v_win'][:, T - wk:], P['ckv'], P['kr'])
    return y, state


def even_sample(x, ln, W, km, vm, bias_t, caches, li, page_table):
    c_cmp_k, c_cmp_v, c_sel_k, c_sel_v, s_win_k, s_win_v, c_ckv, c_kr = caches
    B, S = x.shape[:2]
    n_pages = page_table.shape[1]
    past = n_pages * PAGE_SIZE
    total = past + S
    pos = past + jnp.arange(S, dtype=jnp.int32)
    pt_flat = page_table.reshape(-1)
    P = even_project(x, ln, W, pos)
    l_pad = -(-total // SEL_BLK) * SEL_BLK
    padn = ((0, 0), (0, l_pad - total), (0, 0), (0, 0))
    kc, vc = _nsa_compress_sample((c_cmp_k, c_cmp_v), li, pt_flat, B, n_pages,
                                  (jnp.pad(P['raw_kc'], padn), jnp.pad(P['raw_vc'], padn)), W)
    nsb = l_pad // SEL_BLK
    o_c, idx = _nsa_compressed_and_select(P['q'], pos, kc, vc, nsb, bias_t)
    nbp = past // SEL_BLK
    sel = jnp.any(idx[..., None] == jnp.arange(nbp), axis=-2)
    sel_rows = jnp.broadcast_to(jnp.transpose(sel, (0, 2, 1, 3))[:, :, None], (B, NSA_G, NSA_HPG, S, nbp))
    o_s = _gqa_sample_attention(P['q'], sel_rows.reshape(B, NSA_H * S, nbp), c_sel_k, c_sel_v, li, pt_flat, n_pages,
                                SEL_BLK, P['k_sel'], P['v_sel'], bias_t, NSA_G, "nsa_selected_sample")
    kw = jnp.concatenate([s_win_k[li], P['k_win']], axis=1)
    vw = jnp.concatenate([s_win_v[li], P['v_win']], axis=1)
    wb = s_win_k.shape[2]
    kw_pos = past - wb + jnp.arange(wb + S)
    o_w = _nsa_window_sample(P['q'], pos, kw, vw, kw_pos, bias_t)
    g = P['gates'].reshape(B, S, NSA_G, NSA_HPG, 3, 1)
    o_nsa = (g[..., 0, :] * o_c + g[..., 1, :] * o_s.reshape(B, S, NSA_G, NSA_HPG, HD) + g[..., 2, :] * o_w)
    o_nsa = o_nsa.reshape(B, S, NSA_H * HD)
    o_mla = _mla_sample(P, c_ckv, c_kr, li, pt_flat, n_pages, W)
    o_mem = mem_attend(P['qmem'], km, vm)
    y = even_finish(x, P, o_nsa, o_mla, o_mem, W)
    wk = min(NSA_WINDOW, total)
    n_w = kw.shape[1]
    state = (P['raw_kc'], P['raw_vc'], P['k_sel'], P['v_sel'], kw[:, n_w - wk:], vw[:, n_w - wk:], P['ckv'], P['kr'])
    return y, state


def odd_project(x, ln, W):
    B, T = x.shape[:2]
    h = _matmul(x.reshape(B * T, -1), W['w_in'], gain=ln).reshape(B, T, -1)
    mq, mkv, mz, memq, memz = split_cols(h, ODD_SIZES)
    kv = mkv.reshape(B, T, 2, MOBA_KVH, HD)
    return {
        'q': rms_norm(mq.reshape(B, T, MOBA_H, HD), W['moba_qn']),
        'k': rms_norm(kv[:, :, 0], W['moba_kn']), 'v': kv[:, :, 1],
        'qmem': rms_norm(memq.reshape(B, T, MEM_H, HD), W['mem_qn']),
        'z': (mz, memz),
    }


def odd_finish(x, P, o_moba, o_mem, W):
    z_moba, z_mem = P['z']
    mixed = jnp.concatenate([o_moba * jax.nn.silu(z_moba), o_mem * jax.nn.silu(z_mem)], axis=-1)
    B, T, D = x.shape
    return _matmul(mixed.reshape(B * T, -1), W['w_out'], residual=x.reshape(B * T, D)).reshape(B, T, D)


def _moba_prompt(P, bias_t):
    q, k, v = P['q'], P['k'], P['v']
    B, T = q.shape[:2]
    assert T % MOBA_BLK == 0 and ATT_TILE == MOBA_BLK
    nbm = T // MOBA_BLK
    t = min(ATT_TILE, T)
    own = jnp.arange(T) // MOBA_BLK
    blk = jnp.arange(nbm)
    kmean = k.astype(jnp.float32).reshape(B, nbm, MOBA_BLK, MOBA_KVH, HD).sum(axis=2) / MOBA_BLK
    kmean_h = jnp.repeat(kmean, MOBA_HPG, axis=2)
    sel = (blk[None, :] == own[:, None])[None, :, None, :] | _moba_pick_blocks(q, jnp.arange(T), kmean_h)
    negmask = jnp.where(sel, 0.0, NEG).astype(jnp.bfloat16)
    negmask = jnp.transpose(negmask.reshape(B, T, MOBA_KVH, MOBA_HPG, nbm), (0, 2, 3, 1, 4))
    qq = _heads_first(q * HD ** -0.5, MOBA_KVH).astype(jnp.bfloat16)
    o = _flash(qq, _heads_first(k, MOBA_KVH).astype(jnp.bfloat16), _heads_first(v, MOBA_KVH)[:, :, 0].astype(jnp.bfloat16),
               negmask, _expand_onehot(T, MOBA_BLK, nbm, t),
               _bias_tiles(bias_t, MOBA_HPG, t, min(T5_NEAR_TILES, T // t)), mode='causal', name="moba_prompt")
    return _heads_last(o).reshape(B, T, MOBA_H * HD)


def odd_prompt(x, ln, W, km, vm, bias_t):
    P = odd_project(x, ln, W)
    o_moba = _moba_prompt(P, bias_t)
    y = odd_finish(x, P, o_moba, _mem_attend_prompt(P['qmem'], km, vm), W)
    return y, (P['k'], P['v'])


def odd_sample(x, ln, W, km, vm, bias_t, c_k, c_v, li, page_table):
    B, S = x.shape[:2]
    n_pages = page_table.shape[1]
    past = n_pages * PAGE_SIZE
    pt_flat = page_table.reshape(-1)
    P = odd_project(x, ln, W)
    L, n_pool = c_k.shape[:2]
    page_sum = _paged_sum(c_k.reshape(L, n_pool, PAGE_SIZE, MOBA_KVH * HD), li, pt_flat, B, n_pages)
    kmean = moba_means_paged(page_sum.reshape(B, n_pages, MOBA_KVH, HD), n_pages, P['k'])
    kmean_h = jnp.repeat(kmean, MOBA_HPG, axis=2)
    nbp = past // MOBA_BLK
    picked = _moba_pick_blocks(P['q'], past + jnp.arange(S), kmean_h)[..., :nbp]
    sel_rows = jnp.transpose(picked, (0, 2, 1, 3)).reshape(B, MOBA_H * S, nbp)
    o_moba = _gqa_sample_attention(P['q'], sel_rows, c_k, c_v, li, pt_flat, n_pages, MOBA_BLK, P['k'], P['v'], bias_t,
                                   MOBA_KVH, "moba_sample").reshape(B, S, MOBA_H * HD)
    y = odd_finish(x, P, o_moba, mem_attend(P['qmem'], km, vm), W)
    return y, (P['k'], P['v'])


def kernel(x_prompt, x_sample, cache_nsa_cmp_k, cache_nsa_cmp_v, cache_nsa_sel_k, cache_nsa_sel_v, state_nsa_win_k, state_nsa_win_v, cache_mla_ckv, cache_mla_krope, cache_moba_k, cache_moba_v, cache_mem_k, cache_mem_v, page_table, mem_prompt, rel_bias, ln_g, mem_norm_g, mem_wkv, mem_qn, mem_kn, e_w_in, e_w_out, nsa_qn, nsa_kn, nsa_cmp_pe, nsa_cmp_w1, nsa_cmp_w2, mla_cqn, mla_wuq, mla_ckvn, mla_wuk, mla_wuv, mla_qn, mla_kn, o_w_in, o_w_out, moba_qn, moba_kn):
    xp, xs = x_prompt, x_sample
    depth = ln_g.shape[0]
    even_caches = (cache_nsa_cmp_k, cache_nsa_cmp_v, cache_nsa_sel_k, cache_nsa_sel_v,
                   state_nsa_win_k, state_nsa_win_v, cache_mla_ckv, cache_mla_krope)
    even_p, even_s, odd_p, odd_s, mem_k_p, mem_v_p = [], [], [], [], [], []
    for i in range(depth):
        km_p, vm_p = mem_kv(mem_prompt, mem_norm_g[i], mem_wkv[i], mem_kn[i])
        mem_k_p.append(km_p)
        mem_v_p.append(vm_p)
        li = i // 2
        if i % 2 == 0:
            W = {'w_in': e_w_in[li], 'w_out': e_w_out[li], 'nsa_qn': nsa_qn[li], 'nsa_kn': nsa_kn[li],
                 'cmp_pe': nsa_cmp_pe[li], 'cmp_w1': nsa_cmp_w1[li], 'cmp_w2': nsa_cmp_w2[li],
                 'mla_cqn': mla_cqn[li], 'mla_wuq': mla_wuq[li], 'mla_ckvn': mla_ckvn[li],
                 'mla_wuk': mla_wuk[li], 'mla_wuv': mla_wuv[li], 'mla_qn': mla_qn[li], 'mla_kn': mla_kn[li],
                 'mem_qn': mem_qn[i]}
            xp, st_p = even_prompt(xp, ln_g[i], W, km_p, vm_p, rel_bias)
            xs, st_s = even_sample(xs, ln_g[i], W, cache_mem_k[i], cache_mem_v[i], rel_bias, even_caches, li, page_table)
            even_p.append(st_p)
            even_s.append(st_s)
        else:
            W = {'w_in': o_w_in[li], 'w_out': o_w_out[li], 'moba_qn': moba_qn[li], 'moba_kn': moba_kn[li],
                 'mem_qn': mem_qn[i]}
            xp, st_p = odd_prompt(xp, ln_g[i], W, km_p, vm_p, rel_bias)
            xs, st_s = odd_sample(xs, ln_g[i], W, cache_mem_k[i], cache_mem_v[i], rel_bias,
                                  cache_moba_k, cache_moba_v, li, page_table)
            odd_p.append(st_p)
            odd_s.append(st_s)

    def stk(lst, j):
        return jnp.stack([t[j] for t in lst])

    return (xp, xs,
            stk(even_p, 0), stk(even_p, 1), stk(even_p, 2), stk(even_p, 3),
            stk(even_p, 4), stk(even_p, 5), stk(even_p, 6), stk(even_p, 7),
            stk(odd_p, 0), stk(odd_p, 1), jnp.stack(mem_k_p), jnp.stack(mem_v_p),
            stk(even_s, 0), stk(even_s, 1), stk(even_s, 2), stk(even_s, 3),
            stk(even_s, 4), stk(even_s, 5), stk(even_s, 6), stk(even_s, 7),
            stk(odd_s, 0), stk(odd_s, 1))
```

```python
import functools
import math

import jax
import jax.numpy as jnp
import numpy as np
from jax import lax
from jax.experimental import pallas as pl
from jax.experimental.pallas import tpu as pltpu

PAGE_SIZE = 128
HD = 64
N_BIAS_HEADS = 8
NSA_H = N_BIAS_HEADS
NSA_G = 2
NSA_HPG = NSA_H // NSA_G
CMP_STRIDE = 16
CMP_LEN = 2 * CMP_STRIDE
CMP_HID = 128
SEL_BLK = 64
SEL_TOPK = 16
NSA_WINDOW = 512
MLA_H = 4
Q_LORA = 256
KV_LORA = 128
NOPE_D = 64
ROPE_D = 32
QK_D = NOPE_D + ROPE_D
V_D = 64
ROPE_THETA = 10000.0
MOBA_H = N_BIAS_HEADS
MOBA_KVH = 2
MOBA_HPG = MOBA_H // MOBA_KVH
MOBA_BLK = 256
MOBA_TOPK = 3
MEM_H = 4
N_BUCKETS = 32
T5_MAX_DIST = 2048
Q_BLOCK = 128
EVEN_SIZES = (NSA_H * HD, 6 * NSA_G * HD, 3 * NSA_H, NSA_H * HD, Q_LORA, KV_LORA, ROPE_D, MLA_H * V_D, MEM_H * HD, MEM_H * HD)
ODD_SIZES = (MOBA_H * HD, 2 * MOBA_KVH * HD, MOBA_H * HD, MEM_H * HD, MEM_H * HD)
EPS = 1e-6
NEG = -1e30
FORCE = 1e9

VMEM_LIMIT_BYTES = 56 * 1024 * 1024
LANES = 128
ATT_TILE = 256
T5_SATURATED_DIST = 1536
assert T5_SATURATED_DIST > (N_BUCKETS // 2) * (T5_MAX_DIST / (N_BUCKETS // 2)) ** (15 / 16) + 16
T5_NEAR_TILES = 8
assert (T5_NEAR_TILES - 1) * ATT_TILE - (ATT_TILE - 1) >= T5_SATURATED_DIST


def _mm_kernel(*refs, has_gain, has_res):
    it = iter(refs)
    x_ref = next(it)
    g_ref = next(it) if has_gain else None
    w_ref = next(it)
    r_ref = next(it) if has_res else None
    o_ref = next(it)
    x = x_ref[...]
    if has_gain:
        x = x * lax.rsqrt(jnp.mean(x * x, axis=-1, keepdims=True) + EPS) * g_ref[...]
    acc = jnp.dot(x.astype(jnp.bfloat16), w_ref[...], preferred_element_type=jnp.float32)
    if has_res:
        acc = acc + r_ref[...]
    o_ref[...] = acc


def _matmul(x, w, gain=None, residual=None):
    M, K = x.shape
    N = w.shape[1]
    n_pad = -(-N // LANES) * LANES
    wb = jnp.pad(w, ((0, 0), (0, n_pad - N))).astype(jnp.bfloat16)
    tm = min(512, M)
    assert M % tm == 0 and tm % 8 == 0
    args = [x]
    specs = [pl.BlockSpec((tm, K), lambda i: (i, 0))]
    if gain is not None:
        args.append(gain.reshape(1, K).astype(jnp.float32))
        specs.append(pl.BlockSpec((1, K), lambda i: (0, 0)))
    args.append(wb)
    specs.append(pl.BlockSpec((K, n_pad), lambda i: (0, 0)))
    if residual is not None:
        assert n_pad == N
        args.append(residual)
        specs.append(pl.BlockSpec((tm, n_pad), lambda i: (i, 0)))
    out = pl.pallas_call(
        functools.partial(_mm_kernel, has_gain=gain is not None, has_res=residual is not None),
        grid=(M // tm,),
        in_specs=specs,
        out_specs=pl.BlockSpec((tm, n_pad), lambda i: (i, 0)),
        out_shape=jax.ShapeDtypeStruct((M, n_pad), jnp.float32),
        compiler_params=pltpu.CompilerParams(dimension_semantics=("arbitrary",), vmem_limit_bytes=VMEM_LIMIT_BYTES),
        name="rmsnorm_matmul",
    )(*args)
    return out[:, :N] if n_pad != N else out


def _flash_kernel(*refs, R, RK, RM, t, nk, nd, mode):
    it = iter(refs)
    q_ref, k_ref, v_ref = next(it), next(it), next(it)
    nm_ref = e_ref = bias_ref = None
    if RM:
        nm_ref, e_ref = next(it), next(it)
    if mode != 'full':
        bias_ref = next(it)
    o_ref, m_sc, l_sc, acc_sc = next(it), next(it), next(it), next(it)
    qt = pl.program_id(2)
    m_sc[...] = jnp.full(m_sc.shape, NEG, jnp.float32)
    l_sc[...] = jnp.zeros(l_sc.shape, jnp.float32)
    acc_sc[...] = jnp.zeros(acc_sc.shape, jnp.float32)
    q = q_ref[0, 0, 0]

    def step(kt, with_bias):
        ks = pl.ds(pl.multiple_of(kt * t, t), t)
        if RK == 1:
            s = jnp.dot(k_ref[0, 0, 0, ks, :], q, preferred_element_type=jnp.float32)
        else:
            s = jnp.concatenate(
                [jnp.dot(k_ref[0, 0, r, ks, :], q[:, r * t:(r + 1) * t], preferred_element_type=jnp.float32)
                 for r in range(R)], axis=1)
        if RM:
            mexp = jnp.dot(e_ref[kt], nm_ref[0, 0, 0], preferred_element_type=jnp.float32)
            s = s + (mexp if RM == R else jnp.tile(mexp, (1, R)))
        if with_bias:
            s = s + bias_ref[0, qt - kt]
        m_prev = m_sc[...]
        m_new = jnp.maximum(m_prev, jnp.max(s, axis=0, keepdims=True))
        alpha = jnp.exp(m_prev - m_new)
        p = jnp.exp(s - m_new)
        l_sc[...] = alpha * l_sc[...] + jnp.sum(p, axis=0, keepdims=True)
        acc_sc[...] = alpha * acc_sc[...] + jnp.dot(v_ref[0, 0, kt], p.astype(jnp.bfloat16),
                                                     preferred_element_type=jnp.float32)
        m_sc[...] = m_new

    def loop(lo, hi, with_bias):
        def body(kt, c):
            step(kt, with_bias)
            return c
        lax.fori_loop(lo, hi, body, 0)

    if mode == 'full':
        for kt in range(nk):
            step(kt, False)
    else:
        near_lo = jnp.maximum(qt - (nd - 1), 0)
        if mode == 'causal':
            loop(0, near_lo, False)
        loop(near_lo, qt + 1, True)
    o_ref[0, 0, 0] = acc_sc[...] / l_sc[...]


def _flash(q, k, v, negmask=None, expand=None, bias=None, *, mode, name):
    B, G, R, T, dk = q.shape
    RK, Tk = k.shape[2], k.shape[3]
    dv = v.shape[-1]
    t = min(ATT_TILE, T)
    assert T % t == 0 and Tk % t == 0 and (mode == 'full' or Tk == T)
    nq, nk = T // t, Tk // t
    RM = 0 if negmask is None else negmask.shape[2]
    nd = 0 if bias is None else bias.shape[1]
    q_t = jnp.transpose(q.reshape(B, G, R, nq, t, dk), (0, 1, 3, 5, 2, 4)).reshape(B, G, nq, dk, R * t)
    v_t = jnp.transpose(v.reshape(B, G, nk, t, dv), (0, 1, 2, 4, 3))
    args = [q_t, k, v_t]
    specs = [pl.BlockSpec((1, 1, 1, dk, R * t), lambda b, g, i: (b, g, i, 0, 0)),
             pl.BlockSpec((1, 1, RK, Tk, dk), lambda b, g, i: (b, g, 0, 0, 0)),
             pl.BlockSpec((1, 1, nk, dv, t), lambda b, g, i: (b, g, 0, 0, 0))]
    if RM:
        nb = negmask.shape[-1]
        nm_t = jnp.transpose(negmask.reshape(B, G, RM, nq, t, nb), (0, 1, 3, 5, 2, 4)).reshape(B, G, nq, nb, RM * t)
        e_t = jnp.swapaxes(expand, 1, 2)
        args += [nm_t, e_t]
        specs += [pl.BlockSpec((1, 1, 1, nb, RM * t), lambda b, g, i: (b, g, i, 0, 0)),
                  pl.BlockSpec(e_t.shape, lambda b, g, i: (0, 0, 0))]
    if mode != 'full':
        args.append(jnp.swapaxes(bias, 2, 3))
        specs.append(pl.BlockSpec((1, nd, t, R * t), lambda b, g, i: (g, 0, 0, 0)))
    o_t = pl.pallas_call(
        functools.partial(_flash_kernel, R=R, RK=RK, RM=RM, t=t, nk=nk, nd=nd, mode=mode),
        grid=(B, G, nq),
        in_specs=specs,
        out_specs=pl.BlockSpec((1, 1, 1, dv, R * t), lambda b, g, i: (b, g, i, 0, 0)),
        out_shape=jax.ShapeDtypeStruct((B, G, nq, dv, R * t), jnp.float32),
        scratch_shapes=[pltpu.VMEM((1, R * t), jnp.float32), pltpu.VMEM((1, R * t), jnp.float32),
                        pltpu.VMEM((dv, R * t), jnp.float32)],
        compiler_params=pltpu.CompilerParams(dimension_semantics=("arbitrary", "arbitrary", "arbitrary"),
                                             vmem_limit_bytes=VMEM_LIMIT_BYTES),
        name=name,
    )(*args)
    return jnp.transpose(o_t.reshape(B, G, nq, dv, R, t), (0, 1, 4, 2, 5, 3)).reshape(B, G, R, T, dv)


def _page_specs(n_per_step, n_pages, li, block):
    def index_map(b, c, pt_ref, *, j):
        return (li, pt_ref[b * n_pages + c * n_per_step + j]) + (0,) * (len(block) - 2)
    return [pl.BlockSpec(block, functools.partial(index_map, j=j)) for j in range(n_per_step)]


def _pages_per_step(n_pages):
    pp = 16 if n_pages % 16 == 0 else 8
    assert n_pages % pp == 0
    return pp


_NT = (((1,), (1,)), ((), ()))


def _softmax_update(s, v, m_sc, l_sc, acc_sc, values_transposed=False):
    m_prev = m_sc[...]
    m_new = jnp.maximum(m_prev, jnp.max(s, axis=-1, keepdims=True))
    alpha = jnp.exp(m_prev - m_new)
    p = jnp.exp(s - m_new)
    l_sc[...] = alpha * l_sc[...] + jnp.sum(p, axis=-1, keepdims=True)
    if values_transposed:
        pv = lax.dot_general(p.astype(jnp.bfloat16), v, _NT, preferred_element_type=jnp.float32)
    else:
        pv = jnp.dot(p.astype(jnp.bfloat16), v, preferred_element_type=jnp.float32)
    acc_sc[...] = alpha * acc_sc[...] + pv
    m_sc[...] = m_new


def _paged_attn_kernel(pt_ref, q_ref, *refs, pp, nch, nbc):
    k_refs, v_refs = refs[:pp], refs[pp:2 * pp]
    nm_ref, e_ref, bias_ref, knew_ref, vnew_ref, bnew_ref, o_ref, m_sc, l_sc, acc_sc = refs[2 * pp:]
    c = pl.program_id(1)

    @pl.when(c == 0)
    def _():
        m_sc[...] = jnp.full(m_sc.shape, NEG, jnp.float32)
        l_sc[...] = jnp.zeros(l_sc.shape, jnp.float32)
        acc_sc[...] = jnp.zeros(acc_sc.shape, jnp.float32)

    q = q_ref[0]
    k = jnp.concatenate([r[0, 0] for r in k_refs], axis=1).astype(jnp.bfloat16)
    v = jnp.concatenate([r[0, 0] for r in v_refs], axis=1).astype(jnp.bfloat16)
    s = jnp.dot(q, k, preferred_element_type=jnp.float32)
    s = s + jnp.dot(nm_ref[0], e_ref[c], preferred_element_type=jnp.float32)
    near = (c >= nch - nbc).astype(jnp.float32)
    s = s + near * bias_ref[0]
    _softmax_update(s, v, m_sc, l_sc, acc_sc, values_transposed=True)

    @pl.when(c == nch - 1)
    def _():
        s_new = jnp.dot(q, knew_ref[0], preferred_element_type=jnp.float32) + bnew_ref[...]
        _softmax_update(s_new, vnew_ref[0], m_sc, l_sc, acc_sc, values_transposed=True)
        o_ref[0] = acc_sc[...] / l_sc[...]


def _paged_attn(q_bd, pool_k, pool_v, li, pt_flat, n_pages, negmask, expand, bias_last, k_new, v_new, bias_new, *, name):
    B, RQ, W = q_bd.shape
    pp = _pages_per_step(n_pages)
    nch = n_pages // pp
    nbc = bias_last.shape[0]
    nb = negmask.shape[-1]
    page = (1, 1, W, PAGE_SIZE)
    in_specs = ([pl.BlockSpec((1, RQ, W), lambda b, c, pt: (b, 0, 0))]
                + _page_specs(pp, n_pages, li, page) + _page_specs(pp, n_pages, li, page)
                + [pl.BlockSpec((1, RQ, nb), lambda b, c, pt: (b, 0, 0)),
                   pl.BlockSpec(expand.shape, lambda b, c, pt: (0, 0, 0)),
                   pl.BlockSpec((1, RQ, pp * PAGE_SIZE), lambda b, c, pt: (jnp.maximum(c - (nch - nbc), 0), 0, 0)),
                   pl.BlockSpec((1, W, PAGE_SIZE), lambda b, c, pt: (b, 0, 0)),
                   pl.BlockSpec((1, W, PAGE_SIZE), lambda b, c, pt: (b, 0, 0)),
                   pl.BlockSpec((RQ, PAGE_SIZE), lambda b, c, pt: (0, 0))])
    return pl.pallas_call(
        functools.partial(_paged_attn_kernel, pp=pp, nch=nch, nbc=nbc),
        grid_spec=pltpu.PrefetchScalarGridSpec(
            num_scalar_prefetch=1, grid=(B, nch), in_specs=in_specs,
            out_specs=pl.BlockSpec((1, RQ, W), lambda b, c, pt: (b, 0, 0)),
            scratch_shapes=[pltpu.VMEM((RQ, 1), jnp.float32), pltpu.VMEM((RQ, 1), jnp.float32),
                            pltpu.VMEM((RQ, W), jnp.float32)]),
        out_shape=jax.ShapeDtypeStruct((B, RQ, W), jnp.float32),
        compiler_params=pltpu.CompilerParams(dimension_semantics=("arbitrary", "arbitrary"),
                                             vmem_limit_bytes=VMEM_LIMIT_BYTES),
        name=name,
    )(pt_flat, q_bd, *([pool_k] * pp), *([pool_v] * pp), negmask, expand, bias_last, k_new, v_new, bias_new)


def _paged_mla_kernel(pt_ref, qn_ref, qr_ref, *refs, pp, nch, n_heads, n_q):
    c_refs, r_refs = refs[:pp], refs[pp:2 * pp]
    (wuk_ref, grope_ref, hsel_ref, cos_ref, sin_ref, cnew_ref, rnew_ref, cosn_ref, sinn_ref, bnew_ref,
     o_ref, m_sc, l_sc, acc_sc) = refs[2 * pp:]
    c = pl.program_id(1)

    @pl.when(c == 0)
    def _():
        m_sc[...] = jnp.full(m_sc.shape, NEG, jnp.float32)
        l_sc[...] = jnp.zeros(l_sc.shape, jnp.float32)
        acc_sc[...] = jnp.zeros(acc_sc.shape, jnp.float32)

    def head_sums(x, sel):
        hi = x.astype(jnp.bfloat16)
        lo = (x - hi.astype(jnp.float32)).astype(jnp.bfloat16)
        return (lax.dot_general(sel, hi, _NT, preferred_element_type=jnp.float32)
                + lax.dot_general(sel, lo, _NT, preferred_element_type=jnp.float32))

    def scores(c32, kr_t, cos, sin):
        cb = c32.astype(jnp.bfloat16)
        kn = jnp.dot(cb, wuk_ref[...], preferred_element_type=jnp.float32)
        ss = head_sums(kn * kn, hsel_ref[...]) + jnp.sum(kr_t * kr_t, axis=0, keepdims=True)
        inv = lax.rsqrt(ss / QK_D + EPS) * QK_D ** -0.5
        x = kr_t * grope_ref[...]
        half = ROPE_D // 2
        x1, x2 = x[:half], x[half:]
        krr = jnp.concatenate([x1 * cos - x2 * sin, x2 * cos + x1 * sin], axis=0).astype(jnp.bfloat16)
        s = (lax.dot_general(qn_ref[0], kn.astype(jnp.bfloat16), _NT, preferred_element_type=jnp.float32)
             + jnp.dot(qr_ref[0], krr, preferred_element_type=jnp.float32))
        n = s.shape[1]
        inv_rows = jnp.concatenate([jnp.broadcast_to(inv[h:h + 1], (n_q, n)) for h in range(n_heads)], axis=0)
        return s * inv_rows, cb

    c32 = jnp.concatenate([r[0, 0] for r in c_refs], axis=0)
    kr32 = jnp.concatenate([r[0, 0] for r in r_refs], axis=1)
    s, cb = scores(c32, kr32, cos_ref[0], sin_ref[0])
    _softmax_update(s, cb, m_sc, l_sc, acc_sc)

    @pl.when(c == nch - 1)
    def _():
        s_new, cb_new = scores(cnew_ref[0], rnew_ref[0], cosn_ref[...], sinn_ref[...])
        _softmax_update(s_new + bnew_ref[...], cb_new, m_sc, l_sc, acc_sc)
        o_ref[0] = acc_sc[...] / l_sc[...]


def _paged_mla(qn_bd, qr, pool_c, pool_r, li, pt_flat, n_pages, wuk, grope, cos, sin, c_new, r_new, cos_new, sin_new,
               bias_new, n_heads):
    B, RQ, _ = qn_bd.shape
    pp = _pages_per_step(n_pages)
    nch = n_pages // pp
    half = ROPE_D // 2
    hsel = (jnp.arange(8)[:, None] == (jnp.arange(n_heads * NOPE_D) // NOPE_D)[None, :]).astype(jnp.bfloat16)
    const = lambda shape: pl.BlockSpec(shape, lambda b, c, pt: (0,) * len(shape))
    per_b = lambda shape: pl.BlockSpec(shape, lambda b, c, pt: (b,) + (0,) * (len(shape) - 1))
    in_specs = ([per_b((1, RQ, n_heads * NOPE_D)), per_b((1, RQ, ROPE_D))]
                + _page_specs(pp, n_pages, li, (1, 1, PAGE_SIZE, KV_LORA))
                + _page_specs(pp, n_pages, li, (1, 1, ROPE_D, PAGE_SIZE))
                + [const(wuk.shape), const((ROPE_D, 1)), const(hsel.shape),
                   pl.BlockSpec((1, half, pp * PAGE_SIZE), lambda b, c, pt: (c, 0, 0)),
                   pl.BlockSpec((1, half, pp * PAGE_SIZE), lambda b, c, pt: (c, 0, 0)),
                   per_b((1, PAGE_SIZE, KV_LORA)), per_b((1, ROPE_D, PAGE_SIZE)),
                   const((half, PAGE_SIZE)), const((half, PAGE_SIZE)), const((RQ, PAGE_SIZE))])
    return pl.pallas_call(
        functools.partial(_paged_mla_kernel, pp=pp, nch=nch, n_heads=n_heads, n_q=RQ // n_heads),
        grid_spec=pltpu.PrefetchScalarGridSpec(
            num_scalar_prefetch=1, grid=(B, nch), in_specs=in_specs,
            out_specs=per_b((1, RQ, KV_LORA)),
            scratch_shapes=[pltpu.VMEM((RQ, 1), jnp.float32), pltpu.VMEM((RQ, 1), jnp.float32),
                            pltpu.VMEM((RQ, KV_LORA), jnp.float32)]),
        out_shape=jax.ShapeDtypeStruct((B, RQ, KV_LORA), jnp.float32),
        compiler_params=pltpu.CompilerParams(dimension_semantics=("arbitrary", "arbitrary"),
                                             vmem_limit_bytes=VMEM_LIMIT_BYTES),
        name="mla_sample",
    )(pt_flat, qn_bd, qr, *([pool_c] * pp), *([pool_r] * pp), wuk, grope, hsel, cos, sin, c_new, r_new,
      cos_new, sin_new, bias_new)


def _paged_cmp_proj_kernel(pt_ref, *refs, pp, stride):
    w_ref, o_ref, x_sc = refs[pp:]
    page = x_sc.shape[0] // pp
    for j in range(pp):
        x_sc[j * page:(j + 1) * page, :] = refs[j][0, 0].T
    chunks = x_sc.shape[0] // stride
    acc = jnp.zeros(o_ref.shape[1:], jnp.float32)
    for s in range(stride):
        rows = x_sc[pl.ds(s, chunks, stride=stride), :].astype(jnp.bfloat16)
        acc = acc + jnp.dot(rows, w_ref[s], preferred_element_type=jnp.float32)
    o_ref[0] = acc


def _paged_cmp_proj(pool, li, pt_flat, B, n_pages, w_rows):
    pp = _pages_per_step(n_pages)
    stride, W, n_out = w_rows.shape
    cpp = PAGE_SIZE // stride
    return pl.pallas_call(
        functools.partial(_paged_cmp_proj_kernel, pp=pp, stride=stride),
        grid_spec=pltpu.PrefetchScalarGridSpec(
            num_scalar_prefetch=1, grid=(B, n_pages // pp),
            in_specs=_page_specs(pp, n_pages, li, (1, 1, W, PAGE_SIZE))
            + [pl.BlockSpec(w_rows.shape, lambda b, c, pt: (0, 0, 0))],
            out_specs=pl.BlockSpec((1, pp * cpp, n_out), lambda b, c, pt: (b, c, 0)),
            scratch_shapes=[pltpu.VMEM((pp * PAGE_SIZE, W), jnp.float32)]),
        out_shape=jax.ShapeDtypeStruct((B, n_pages * cpp, n_out), jnp.float32),
        compiler_params=pltpu.CompilerParams(dimension_semantics=("arbitrary", "arbitrary"),
                                             vmem_limit_bytes=VMEM_LIMIT_BYTES),
        name="nsa_compress_sample",
    )(pt_flat, *([pool] * pp), w_rows)


def _paged_sum_kernel(pt_ref, *refs, pp):
    o_ref = refs[pp]
    o_ref[0, 0] = jnp.concatenate([jnp.sum(r[0, 0], axis=1, keepdims=True) for r in refs[:pp]], axis=1)


def _paged_sum(pool, li, pt_flat, B, n_pages):
    pp = _pages_per_step(n_pages)
    W = pool.shape[2]
    out = pl.pallas_call(
        functools.partial(_paged_sum_kernel, pp=pp),
        grid_spec=pltpu.PrefetchScalarGridSpec(
            num_scalar_prefetch=1, grid=(B, n_pages // pp),
            in_specs=_page_specs(pp, n_pages, li, (1, 1, W, PAGE_SIZE)),
            out_specs=pl.BlockSpec((1, 1, W, pp), lambda b, c, pt: (b, c, 0, 0))),
        out_shape=jax.ShapeDtypeStruct((B, n_pages // pp, W, pp), jnp.float32),
        compiler_params=pltpu.CompilerParams(dimension_semantics=("arbitrary", "arbitrary"),
                                             vmem_limit_bytes=VMEM_LIMIT_BYTES),
        name="moba_page_sums",
    )(pt_flat, *([pool] * pp))
    return jnp.transpose(out, (0, 1, 3, 2)).reshape(B, n_pages, W)


def rms_norm(x, g):
    xf = x.astype(jnp.float32)
    y = xf * lax.rsqrt(jnp.mean(xf * xf, axis=-1, keepdims=True) + EPS)
    return (y * g.astype(jnp.float32)).astype(x.dtype)


def _project_split(x2d, w, sizes, gain):
    starts = np.cumsum((0,) + tuple(sizes))[:-1]
    padded = [-(-n // LANES) * LANES for n in sizes]
    w_al = jnp.concatenate([jnp.pad(w[:, o:o + n], ((0, 0), (0, p - n))) for o, n, p in zip(starts, sizes, padded)], axis=1)
    h = _matmul(x2d, w_al, gain=gain)
    al = np.cumsum([0] + padded)[:-1]
    return [h[:, o:o + n] for o, n in zip(al, sizes)]


def masked_softmax(s, mask):
    p = jax.nn.softmax(jnp.where(mask, s, NEG), axis=-1)
    return p * mask


def t5_bucket(dist):
    dist = jnp.maximum(dist, 0)
    exact = N_BUCKETS // 2
    far = exact + (jnp.log(jnp.maximum(dist, 1).astype(jnp.float32) / exact)
                   / math.log(T5_MAX_DIST / exact) * (N_BUCKETS - exact)).astype(jnp.int32)
    return jnp.where(dist < exact, dist, jnp.minimum(far, N_BUCKETS - 1))


def rope(x, pos):
    half = ROPE_D // 2
    inv = ROPE_THETA ** (-jnp.arange(half, dtype=jnp.float32) / half)
    ang = pos.astype(jnp.float32)[:, None] * inv
    ang = ang.reshape((ang.shape[0],) + (1,) * (x.ndim - 3) + (half,))
    cos, sin = jnp.cos(ang), jnp.sin(ang)
    xf = x.astype(jnp.float32)
    x1, x2 = xf[..., :half], xf[..., half:]
    return jnp.concatenate([x1 * cos - x2 * sin, x2 * cos + x1 * sin], axis=-1).astype(x.dtype)


def _bias_lookup(rel_bias, bucket):
    onehot = (bucket[..., None] == jnp.arange(N_BUCKETS)).astype(jnp.float32)
    return jnp.einsum('...k,kh->...h', onehot, rel_bias.astype(jnp.float32), precision=lax.Precision.HIGHEST)


def _sample_bias(rel_bias, S, past, keys_per_chunk, nbc):
    H = rel_bias.shape[1]
    rel = rel_bias - rel_bias[N_BUCKETS - 1]
    n = nbc * keys_per_chunk
    s = jnp.arange(S)
    dist = (past + s)[:, None] - (past - n + jnp.arange(n))[None, :]
    b = jnp.transpose(_bias_lookup(rel, t5_bucket(dist)), (2, 0, 1)).reshape(H * S, nbc, keys_per_chunk)
    j = jnp.arange(PAGE_SIZE)
    dn = s[:, None] - j[None, :]
    ok = (dn >= 0) & (j < S)[None, :]
    bn = jnp.where(ok[..., None], _bias_lookup(rel, t5_bucket(dn)), NEG)
    return jnp.transpose(b, (1, 0, 2)), jnp.transpose(bn, (2, 0, 1)).reshape(H * S, PAGE_SIZE)


def _transposed_pages(pool):
    L, n_pool, page = pool.shape[:3]
    return jnp.moveaxis(pool.reshape(L, n_pool, page, -1), 2, 3)


def _block_diag_rows(q, groups):
    B, S, H, D = q.shape
    qt = jnp.transpose(q, (0, 2, 1, 3)).reshape(B, groups, (H // groups) * S, D)
    out = qt[:, :, :, None, :] * jnp.eye(groups, dtype=q.dtype)[None, :, None, :, None]
    return out.reshape(B, H * S, groups * D)


def _block_diag_extract(o, groups, S):
    B, RQ, W = o.shape
    D = W // groups
    o5 = o.reshape(B, groups, RQ // groups, groups, D)
    od = jnp.stack([o5[:, g, :, g, :] for g in range(groups)], axis=1)
    return jnp.transpose(od.reshape(B, RQ // S, S, D), (0, 2, 1, 3))


def _bias_tiles(rel_bias, heads_per_group, t, nd, window=None):
    i = jnp.arange(t)[:, None]
    j = jnp.arange(t)[None, :]
    dist = jnp.arange(nd)[:, None, None] * t + i - j
    b = _bias_lookup(rel_bias - rel_bias[N_BUCKETS - 1], t5_bucket(dist))
    ok = dist >= 0
    if window is not None:
        ok = ok & (dist < window)
    b = jnp.where(ok[..., None], b, NEG)
    n_heads = rel_bias.shape[1]
    b = jnp.transpose(b, (3, 0, 1, 2)).reshape(n_heads // heads_per_group, heads_per_group, nd, t, t)
    return jnp.transpose(b, (0, 2, 1, 3, 4)).reshape(n_heads // heads_per_group, nd, heads_per_group * t, t).astype(jnp.float32)


def _causal_tiles(n_groups, rows, t):
    ok = jnp.arange(t)[:, None] >= jnp.arange(t)[None, :]
    b = jnp.where(ok, 0.0, NEG).astype(jnp.float32)
    return jnp.broadcast_to(jnp.tile(b, (rows, 1))[None, None], (n_groups, 1, rows * t, t))


def _expand_onehot(n_keys, blk, nb, t):
    key_blk = (jnp.arange(n_keys) // blk).reshape(n_keys // t, 1, t)
    return (key_blk == jnp.arange(nb)[None, :, None]).astype(jnp.bfloat16)


def _heads_first(x, groups):
    B, T, H, D = x.shape
    return jnp.transpose(x.reshape(B, T, groups, H // groups, D), (0, 2, 3, 1, 4))


def _heads_last(o):
    B, G, R, T, D = o.shape
    return jnp.transpose(o, (0, 3, 1, 2, 4)).reshape(B, T, G * R, D)


def cmp_chunk_proj(raw, w1):
    B, L = raw.shape[:2]
    ch = raw.reshape(B, L // CMP_STRIDE, CMP_STRIDE, NSA_G, HD)
    a = jnp.einsum('bnsgd,sdh->bngh', ch, w1[:CMP_STRIDE])
    b = jnp.einsum('bnsgd,sdh->bngh', ch, w1[CMP_STRIDE:])
    return a, b


def cmp_finish(a, b, pe, w1, w2):
    pe_h = jnp.einsum('sd,sdh->h', pe, w1)
    h = jax.nn.gelu(a[:, :-1] + b[:, 1:] + pe_h)
    return jnp.einsum('bngh,hd->bngd', h, w2)


def nsa_compress(raw_k_parts, raw_v_parts, W):
    out = []
    for j, parts in enumerate((raw_k_parts, raw_v_parts)):
        w1 = W['cmp_w1'][j]
        ab = [cmp_chunk_proj(r, w1) for r in parts]
        a = jnp.concatenate([t[0] for t in ab], axis=1)
        b = jnp.concatenate([t[1] for t in ab], axis=1)
        out.append(cmp_finish(a, b, W['cmp_pe'][j], w1, W['cmp_w2'][j]))
    return rms_norm(out[0], W['nsa_kn'][0]), out[1]


def sel_block_scores(imp, nsb):
    r = SEL_BLK // CMP_STRIDE
    front = CMP_LEN // CMP_STRIDE - 1
    nc = imp.shape[-1]
    imp = jnp.pad(imp, [(0, 0)] * (imp.ndim - 1) + [(front, r * nsb - nc)])
    score = 0.0
    for o in range(-front, r):
        lo = o * CMP_STRIDE
        w = max(0, min(lo + CMP_LEN, SEL_BLK) - max(lo, 0)) / CMP_LEN
        start = o + front
        score = score + w * imp[..., start:start + r * (nsb - 1) + 1:r]
    return score


def _topk_mask(score, k):
    iota = jnp.arange(score.shape[-1])
    mask = jnp.zeros(score.shape, bool)
    for _ in range(k):
        hit = iota == jnp.argmax(score, axis=-1)[..., None]
        mask = mask | hit
        score = jnp.where(hit, -jnp.inf, score)
    return mask


def _nsa_compressed_and_select(q, q_pos, kc, vc, nsb, bias_t):
    B, Q = q.shape[:2]
    scale = HD ** -0.5
    f32 = jnp.float32
    qg = q.reshape(B, Q, NSA_G, NSA_HPG, HD)
    nc = kc.shape[1]
    dist_c = q_pos[:, None] - (jnp.arange(nc) * CMP_STRIDE + CMP_LEN - 1)[None, :]
    bias_c = _bias_lookup(bias_t, t5_bucket(dist_c)).reshape(Q, nc, NSA_G, NSA_HPG)
    bias_c = jnp.transpose(bias_c, (0, 2, 3, 1))[None]
    s_c =jnp.einsum('bqgpd,bngd->bqgpn', qg, kc, preferred_element_type=f32) * scale + bias_c
    p_c = masked_softmax(s_c, (dist_c >= 0)[None, :, None, None, :])
    o_c = jnp.einsum('bqgpn,bngd->bqgpd', p_c.astype(vc.dtype), vc)
    score = sel_block_scores(p_c.sum(axis=3), nsb)
    blk = jnp.arange(nsb)[None, :]
    own = (q_pos // SEL_BLK)[:, None]
    forced = (blk == 0) | (blk == own) | (blk == own - 1)
    score = jnp.where(forced[None, :, None, :], FORCE,
                      jnp.where((blk <= own)[None, :, None, :], score, -jnp.inf))
    sel = _topk_mask(score, min(SEL_TOPK, nsb)) & (blk <= own)[None, :, None, :]
    return o_c, sel


def _nsa_window_sample(q, q_pos, kw, vw, kw_pos, bias_t):
    B, Q = q.shape[:2]
    qg = q.reshape(B, Q, NSA_G, NSA_HPG, HD)
    dist_w = q_pos[:, None] - kw_pos[None, :]
    mask_w = (dist_w >= 0) & (dist_w < NSA_WINDOW) & (kw_pos >= 0)[None, :]
    bias_w = _bias_lookup(bias_t, t5_bucket(dist_w)).reshape(Q, kw_pos.shape[0], NSA_G, NSA_HPG)
    bias_w = jnp.transpose(bias_w, (0, 2, 3, 1))[None]
    s_w = jnp.einsum('bqgpd,blgd->bqgpl', qg, kw, preferred_element_type=jnp.float32) * HD ** -0.5 + bias_w
    p_w = masked_softmax(s_w, mask_w[None, :, None, None, :])
    return jnp.einsum('bqgpl,blgd->bqgpd', p_w.astype(vw.dtype), vw)


def _gqa_sample_attention(q, sel_rows, pool_k, pool_v, li, pt_flat, n_pages, blk, k_new, v_new, bias_t, groups, name):
    B, S, H, D = q.shape
    L, n_pool = pool_k.shape[:2]
    past = n_pages * PAGE_SIZE
    assert past % blk == 0 and S <= blk and S <= PAGE_SIZE
    pp = _pages_per_step(n_pages)
    keys = pp * PAGE_SIZE
    nbc = min(n_pages // pp, -(-T5_SATURATED_DIST // keys))
    W = groups * D
    q_bd = _block_diag_rows(q * D ** -0.5, groups).astype(jnp.bfloat16)
    negmask = jnp.where(sel_rows, 0.0, NEG).astype(jnp.bfloat16)
    expand = _expand_onehot(past, blk, past // blk, keys)
    bias_last, bias_new = _sample_bias(bias_t, S, past, keys, nbc)
    padr = ((0, 0), (0, PAGE_SIZE - S), (0, 0))
    kn = jnp.swapaxes(jnp.pad(k_new.reshape(B, S, W), padr), 1, 2).astype(jnp.bfloat16)
    vn = jnp.swapaxes(jnp.pad(v_new.reshape(B, S, W), padr), 1, 2).astype(jnp.bfloat16)
    o = _paged_attn(q_bd, _transposed_pages(pool_k), _transposed_pages(pool_v), li, pt_flat,
                    n_pages, negmask, expand, bias_last, kn, vn, bias_new, name=name)
    return _block_diag_extract(o, groups, S)


def _nsa_compress_sample(pools, li, pt_flat, B, n_pages, new_parts, W):
    out = []
    chunks_per_page = PAGE_SIZE // CMP_STRIDE
    for j, (pool, new) in enumerate(zip(pools, new_parts)):
        w1 = W['cmp_w1'][j]
        w1r = w1.reshape(2, CMP_STRIDE, HD, CMP_HID)
        w_rows = jnp.einsum('wsdh,gk->sgdwkh', w1r, jnp.eye(NSA_G, dtype=w1.dtype))
        w_rows = w_rows.reshape(CMP_STRIDE, NSA_G * HD, 2 * NSA_G * CMP_HID).astype(jnp.bfloat16)
        ab = _paged_cmp_proj(_transposed_pages(pool), li, pt_flat, B, n_pages, w_rows)
        ab = ab.reshape(B, n_pages * chunks_per_page, 2, NSA_G, CMP_HID)
        a_new, b_new = cmp_chunk_proj(new, w1)
        a = jnp.concatenate([ab[:, :, 0], a_new], axis=1)
        b = jnp.concatenate([ab[:, :, 1], b_new], axis=1)
        out.append(cmp_finish(a, b, W['cmp_pe'][j], w1, W['cmp_w2'][j]))
    return rms_norm(out[0], W['nsa_kn'][0]), out[1]


def mla_keys(ckv, kr, k_pos, W):
    kn = jnp.einsum('blc,chd->blhd', ckv, W['mla_wuk'])
    krf = kr.astype(jnp.float32)
    ss = jnp.einsum('blhd,blhd->blh', kn, kn, preferred_element_type=jnp.float32) + jnp.sum(krf * krf, -1)[..., None]
    inv = lax.rsqrt(ss / QK_D + EPS)
    krr = rope(kr * W['mla_kn'][NOPE_D:], k_pos)
    return kn, inv, krr


def _rope_tables(pos):
    half = ROPE_D // 2
    inv = ROPE_THETA ** (-jnp.arange(half, dtype=jnp.float32) / half)
    ang = pos.astype(jnp.float32)[:, None] * inv
    return jnp.cos(ang), jnp.sin(ang)


def _mla_sample(P, c_ckv, c_kr, li, pt_flat, n_pages, W):
    B, S = P['ckv'].shape[:2]
    assert S <= PAGE_SIZE
    past = n_pages * PAGE_SIZE
    pp = _pages_per_step(n_pages)
    qn = _block_diag_rows(P['q_nope'] * W['mla_kn'][:NOPE_D], MLA_H).astype(jnp.bfloat16)
    qr = jnp.transpose(P['q_rope'], (0, 2, 1, 3)).reshape(B, MLA_H * S, ROPE_D).astype(jnp.bfloat16)
    chunked_t = lambda a: jnp.swapaxes(a.reshape(n_pages // pp, pp * PAGE_SIZE, -1), 1, 2)
    cos, sin = [chunked_t(a) for a in _rope_tables(jnp.arange(past))]
    cos_new, sin_new = [a.T for a in _rope_tables(past + jnp.arange(PAGE_SIZE))]
    padr = ((0, 0), (0, PAGE_SIZE - S), (0, 0))
    j = jnp.arange(PAGE_SIZE)[None, :]
    s = jnp.arange(S)[:, None]
    bias_new = jnp.tile(jnp.where((j <= s) & (j < S), 0.0, NEG).astype(jnp.float32), (MLA_H, 1))
    o_lat = _paged_mla(qn, qr, c_ckv, _transposed_pages(c_kr), li, pt_flat, n_pages,
                       W['mla_wuk'].reshape(KV_LORA, MLA_H * NOPE_D).astype(jnp.bfloat16),
                       W['mla_kn'][NOPE_D:].reshape(ROPE_D, 1).astype(jnp.float32), cos, sin,
                       jnp.pad(P['ckv'], padr), jnp.swapaxes(jnp.pad(P['kr'], padr), 1, 2), cos_new, sin_new,
                       bias_new, MLA_H)
    o_lat = jnp.transpose(o_lat.reshape(B, MLA_H, S, KV_LORA), (0, 2, 1, 3))
    return jnp.einsum('bqhc,chd->bqhd', o_lat, W['mla_wuv']).reshape(B, S, MLA_H * V_D)


def _mla_prompt(P, pos, W):
    kn, inv, krr = mla_keys(P['ckv'], P['kr'], pos, W)
    B, T = kn.shape[:2]
    kfull = jnp.concatenate([kn, jnp.broadcast_to(krr[:, :, None, :], (B, T, MLA_H, ROPE_D))], axis=-1)
    kfull = kfull * (inv * QK_D ** -0.5)[..., None]
    qfull = jnp.concatenate([P['q_nope'] * W['mla_kn'][:NOPE_D], P['q_rope']], axis=-1)
    padd = ((0, 0), (0, 0), (0, 0), (0, LANES - QK_D))
    kk = _heads_first(jnp.pad(kfull, padd), 1).astype(jnp.bfloat16)
    qq = _heads_first(jnp.pad(qfull, padd), 1).astype(jnp.bfloat16)
    vv = P['ckv'].astype(jnp.bfloat16)[:, None]
    t = min(ATT_TILE, T)
    o_lat = _flash(qq, kk, vv, bias=_causal_tiles(1, MLA_H, t), mode='causal', name="mla_prompt")
    o_lat = _heads_last(o_lat)
    return jnp.einsum('bqhc,chd->bqhd', o_lat, W['mla_wuv']).reshape(B, T, MLA_H * V_D)


def moba_means_paged(page_sum, n_pages, k_new):
    B = page_sum.shape[0]
    past = n_pages * PAGE_SIZE
    total = past + k_new.shape[1]
    nbm = -(-total // MOBA_BLK)
    pps = MOBA_BLK // PAGE_SIZE
    npp = -(-n_pages // pps) * pps
    page_sum = jnp.pad(page_sum, ((0, 0), (0, npp - n_pages), (0, 0), (0, 0)))
    blk_sum = page_sum.reshape(B, npp // pps, pps, MOBA_KVH, HD).sum(axis=2)
    blk_sum = jnp.pad(blk_sum, ((0, 0), (0, nbm - npp // pps), (0, 0), (0, 0)))
    new_blk = (past + jnp.arange(k_new.shape[1])) // MOBA_BLK
    onehot = (new_blk[:, None] == jnp.arange(nbm)[None, :]).astype(jnp.float32)
    blk_sum = blk_sum + jnp.einsum('bsgd,sn->bngd', k_new.astype(jnp.float32), onehot)
    return blk_sum / MOBA_BLK


def _moba_pick_blocks(q, q_pos, kmean_h):
    nbm = kmean_h.shape[1]
    own = q_pos // MOBA_BLK
    blk = jnp.arange(nbm)
    full_past = (blk[None, :] < own[:, None])[None, :, None, :]
    k_eff = min(MOBA_TOPK, nbm - 1)
    if k_eff <= 0:
        return jnp.zeros(q.shape[:3] + (nbm,), bool)
    gs = jnp.einsum('bqhd,bnhd->bqhn', q, kmean_h, preferred_element_type=jnp.float32)
    gs = jnp.where(full_past, gs, -jnp.inf)
    return _topk_mask(gs, k_eff) & full_past


def mem_kv(mem, g, wkv, kn):
    B, N = mem.shape[:2]
    h = _matmul(mem.reshape(B * N, -1), wkv, gain=g).reshape(B, N, 2, MEM_H, HD)
    return rms_norm(h[:, :, 0], kn), h[:, :, 1]


def mem_attend(q, km, vm):
    B, Q = q.shape[:2]
    s = jnp.einsum('bqhd,bmhd->bhqm', q, km, preferred_element_type=jnp.float32) * HD ** -0.5
    p = jax.nn.softmax(s, axis=-1).astype(vm.dtype)
    return jnp.einsum('bhqm,bmhd->bqhd', p, vm).reshape(B, Q, MEM_H * HD)


def _mem_attend_prompt(q, km, vm):
    B, T = q.shape[:2]
    qq = _heads_first(q * HD ** -0.5, MEM_H).astype(jnp.bfloat16)
    kk = _heads_first(km, MEM_H).astype(jnp.bfloat16)
    vv = _heads_first(vm, MEM_H)[:, :, 0].astype(jnp.bfloat16)
    return _heads_last(_flash(qq, kk, vv, mode='full', name="mem_prompt")).reshape(B, T, MEM_H * HD)


def even_project(x, ln, W, pos):
    B, T = x.shape[:2]
    pieces = _project_split(x.reshape(B * T, -1), W['w_in'], EVEN_SIZES, ln)
    nq, nkv, ngt, nz, cq, ckv, kr, mz, mq, memz = [p.reshape(B, T, -1) for p in pieces]
    kv = nkv.reshape(B, T, 6, NSA_G, HD)
    qm = _matmul(cq.reshape(B * T, -1), W['mla_wuq'], gain=W['mla_cqn']).reshape(B, T, MLA_H, QK_D)
    qm = rms_norm(qm, W['mla_qn'])
    return {
        'q': rms_norm(nq.reshape(B, T, NSA_H, HD), W['nsa_qn']),
        'gates': jax.nn.sigmoid(ngt.reshape(B, T, NSA_H, 3)),
        'raw_kc': kv[:, :, 0], 'raw_vc': kv[:, :, 1],
        'k_sel': rms_norm(kv[:, :, 2], W['nsa_kn'][1]), 'v_sel': kv[:, :, 3],
        'k_win': rms_norm(kv[:, :, 4], W['nsa_kn'][2]), 'v_win': kv[:, :, 5],
        'q_nope': qm[..., :NOPE_D], 'q_rope': rope(qm[..., NOPE_D:], pos),
        'ckv': rms_norm(ckv, W['mla_ckvn']), 'kr': kr,
        'qmem': rms_norm(mq.reshape(B, T, MEM_H, HD), W['mem_qn']),
        'z': (nz, mz, memz),
    }


def even_finish(x, P, o_nsa, o_mla, o_mem, W):
    z_nsa, z_mla, z_mem = P['z']
    mixed = jnp.concatenate([o_nsa * jax.nn.silu(z_nsa), o_mla * jax.nn.silu(z_mla), o_mem * jax.nn.silu(z_mem)], axis=-1)
    B, T, D = x.shape
    return _matmul(mixed.reshape(B * T, -1), W['w_out'], residual=x.reshape(B * T, D)).reshape(B, T, D)


def _nsa_prompt(P, kc, vc, bias_t):
    q = P['q']
    B, T = q.shape[:2]
    nsb = T // SEL_BLK
    t = min(ATT_TILE, T)
    o_c, sel = _nsa_compressed_and_select(q, jnp.arange(T), kc, vc, nsb, bias_t)
    negmask = jnp.where(sel, 0.0, NEG).astype(jnp.bfloat16)
    negmask = jnp.transpose(negmask, (0, 2, 1, 3))[:, :, None]
    qq = _heads_first(q * HD ** -0.5, NSA_G).astype(jnp.bfloat16)
    o_s = _flash(qq, _heads_first(P['k_sel'], NSA_G).astype(jnp.bfloat16),
                 _heads_first(P['v_sel'], NSA_G)[:, :, 0].astype(jnp.bfloat16),
                 negmask, _expand_onehot(T, SEL_BLK, nsb, t),
                 _bias_tiles(bias_t, NSA_HPG, t, min(T5_NEAR_TILES, T // t)), mode='causal', name="nsa_selected_prompt")
    n_win = -(-(NSA_WINDOW - 1) // t) + 1
    o_w = _flash(qq, _heads_first(P['k_win'], NSA_G).astype(jnp.bfloat16),
                 _heads_first(P['v_win'], NSA_G)[:, :, 0].astype(jnp.bfloat16),
                 bias=_bias_tiles(bias_t, NSA_HPG, t, min(n_win, T // t), window=NSA_WINDOW),
                 mode='window', name="nsa_window_prompt")
    g = P['gates'].reshape(B, T, NSA_G, NSA_HPG, 3, 1)
    o_s = _heads_last(o_s).reshape(B, T, NSA_G, NSA_HPG, HD)
    o_w = _heads_last(o_w).reshape(B, T, NSA_G, NSA_HPG, HD)
    o = g[..., 0, :] * o_c + g[..., 1, :] * o_s + g[..., 2, :] * o_w
    return o.reshape(B, T, NSA_H * HD)


def even_prompt(x, ln, W, km, vm, bias_t):
    B, T = x.shape[:2]
    pos = jnp.arange(T, dtype=jnp.int32)
    P = even_project(x, ln, W, pos)
    kc, vc = nsa_compress([P['raw_kc']], [P['raw_vc']], W)
    o_nsa = _nsa_prompt(P, kc, vc, bias_t)
    o_mla = _mla_prompt(P, pos, W)
    o_mem = _mem_attend_prompt(P['qmem'], km, vm)
    y = even_finish(x, P, o_nsa, o_mla, o_mem, W)
    wk = min(NSA_WINDOW, T)
    state = (P['raw_kc'], P['raw_vc'], P['k_sel'], P['v_sel'], P['k_win'][:, T - wk:], P['v_win'][:, T - wk:], P['ckv'], P['kr'])
    return y, state


def even_sample(x, ln, W, km, vm, bias_t, caches, li, page_table):
    c_cmp_k, c_cmp_v, c_sel_k, c_sel_v, s_win_k, s_win_v, c_ckv, c_kr = caches
    B, S = x.shape[:2]
    n_pages = page_table.shape[1]
    past = n_pages * PAGE_SIZE
    total = past + S
    pos = past + jnp.arange(S, dtype=jnp.int32)
    pt_flat = page_table.reshape(-1)
    P = even_project(x, ln, W, pos)
    l_pad = -(-total // SEL_BLK) * SEL_BLK
    padn = ((0, 0), (0, l_pad - total), (0, 0), (0, 0))
    kc, vc = _nsa_compress_sample((c_cmp_k, c_cmp_v), li, pt_flat, B, n_pages,
                                  (jnp.pad(P['raw_kc'], padn), jnp.pad(P['raw_vc'], padn)), W)
    nsb = l_pad // SEL_BLK
    o_c, sel = _nsa_compressed_and_select(P['q'], pos, kc, vc, nsb, bias_t)
    nbp = past // SEL_BLK
    sel = sel[..., :nbp]
    sel_rows = jnp.broadcast_to(jnp.transpose(sel, (0, 2, 1, 3))[:, :, None], (B, NSA_G, NSA_HPG, S, nbp))
    o_s = _gqa_sample_attention(P['q'], sel_rows.reshape(B, NSA_H * S, nbp), c_sel_k, c_sel_v, li, pt_flat, n_pages,
                                SEL_BLK, P['k_sel'], P['v_sel'], bias_t, NSA_G, "nsa_selected_sample")
    kw = jnp.concatenate([s_win_k[li], P['k_win']], axis=1)
    vw = jnp.concatenate([s_win_v[li], P['v_win']], axis=1)
    wb = s_win_k.shape[2]
    kw_pos = past - wb + jnp.arange(wb + S)
    o_w = _nsa_window_sample(P['q'], pos, kw, vw, kw_pos, bias_t)
    g = P['gates'].reshape(B, S, NSA_G, NSA_HPG, 3, 1)
    o_nsa = (g[..., 0, :] * o_c + g[..., 1, :] * o_s.reshape(B, S, NSA_G, NSA_HPG, HD) + g[..., 2, :] * o_w)
    o_nsa = o_nsa.reshape(B, S, NSA_H * HD)
    o_mla = _mla_sample(P, c_ckv, c_kr, li, pt_flat, n_pages, W)
    o_mem = mem_attend(P['qmem'], km, vm)
    y = even_finish(x, P, o_nsa, o_mla, o_mem, W)
    wk = min(NSA_WINDOW, total)
    n_w = kw.shape[1]
    state = (P['raw_kc'], P['raw_vc'], P['k_sel'], P['v_sel'], kw[:, n_w - wk:], vw[:, n_w - wk:], P['ckv'], P['kr'])
    return y, state


def odd_project(x, ln, W):
    B, T = x.shape[:2]
    pieces = _project_split(x.reshape(B * T, -1), W['w_in'], ODD_SIZES, ln)
    mq, mkv, mz, memq, memz = [p.reshape(B, T, -1) for p in pieces]
    kv = mkv.reshape(B, T, 2, MOBA_KVH, HD)
    return {
        'q': rms_norm(mq.reshape(B, T, MOBA_H, HD), W['moba_qn']),
        'k': rms_norm(kv[:, :, 0], W['moba_kn']), 'v': kv[:, :, 1],
        'qmem': rms_norm(memq.reshape(B, T, MEM_H, HD), W['mem_qn']),
        'z': (mz, memz),
    }


def odd_finish(x, P, o_moba, o_mem, W):
    z_moba, z_mem = P['z']
    mixed = jnp.concatenate([o_moba * jax.nn.silu(z_moba), o_mem * jax.nn.silu(z_mem)], axis=-1)
    B, T, D = x.shape
    return _matmul(mixed.reshape(B * T, -1), W['w_out'], residual=x.reshape(B * T, D)).reshape(B, T, D)


def _moba_prompt(P, bias_t):
    q, k, v = P['q'], P['k'], P['v']
    B, T = q.shape[:2]
    assert T % MOBA_BLK == 0 and ATT_TILE == MOBA_BLK
    nbm = T // MOBA_BLK
    t = min(ATT_TILE, T)
    own = jnp.arange(T) // MOBA_BLK
    blk = jnp.arange(nbm)
    kmean = k.astype(jnp.float32).reshape(B, nbm, MOBA_BLK, MOBA_KVH, HD).sum(axis=2) / MOBA_BLK
    kmean_h = jnp.repeat(kmean, MOBA_HPG, axis=2)
    sel = (blk[None, :] == own[:, None])[None, :, None, :] | _moba_pick_blocks(q, jnp.arange(T), kmean_h)
    negmask = jnp.where(sel, 0.0, NEG).astype(jnp.bfloat16)
    negmask = jnp.transpose(negmask.reshape(B, T, MOBA_KVH, MOBA_HPG, nbm), (0, 2, 3, 1, 4))
    qq = _heads_first(q * HD ** -0.5, MOBA_KVH).astype(jnp.bfloat16)
    o = _flash(qq, _heads_first(k, MOBA_KVH).astype(jnp.bfloat16), _heads_first(v, MOBA_KVH)[:, :, 0].astype(jnp.bfloat16),
               negmask, _expand_onehot(T, MOBA_BLK, nbm, t),
               _bias_tiles(bias_t, MOBA_HPG, t, min(T5_NEAR_TILES, T // t)), mode='causal', name="moba_prompt")
    return _heads_last(o).reshape(B, T, MOBA_H * HD)


def odd_prompt(x, ln, W, km, vm, bias_t):
    P = odd_project(x, ln, W)
    o_moba = _moba_prompt(P, bias_t)
    y = odd_finish(x, P, o_moba, _mem_attend_prompt(P['qmem'], km, vm), W)
    return y, (P['k'], P['v'])


def odd_sample(x, ln, W, km, vm, bias_t, c_k, c_v, li, page_table):
    B, S = x.shape[:2]
    n_pages = page_table.shape[1]
    past = n_pages * PAGE_SIZE
    pt_flat = page_table.reshape(-1)
    P = odd_project(x, ln, W)
    page_sum = _paged_sum(_transposed_pages(c_k), li, pt_flat, B, n_pages)
    kmean = moba_means_paged(page_sum.reshape(B, n_pages, MOBA_KVH, HD), n_pages, P['k'])
    kmean_h = jnp.repeat(kmean, MOBA_HPG, axis=2)
    nbp = past // MOBA_BLK
    picked = _moba_pick_blocks(P['q'], past + jnp.arange(S), kmean_h)[..., :nbp]
    sel_rows = jnp.transpose(picked, (0, 2, 1, 3)).reshape(B, MOBA_H * S, nbp)
    o_moba = _gqa_sample_attention(P['q'], sel_rows, c_k, c_v, li, pt_flat, n_pages, MOBA_BLK, P['k'], P['v'], bias_t,
                                   MOBA_KVH, "moba_sample").reshape(B, S, MOBA_H * HD)
    y = odd_finish(x, P, o_moba, mem_attend(P['qmem'], km, vm), W)
    return y, (P['k'], P['v'])


def kernel(x_prompt, x_sample, cache_nsa_cmp_k, cache_nsa_cmp_v, cache_nsa_sel_k, cache_nsa_sel_v, state_nsa_win_k, state_nsa_win_v, cache_mla_ckv, cache_mla_krope, cache_moba_k, cache_moba_v, cache_mem_k, cache_mem_v, page_table, mem_prompt, rel_bias, ln_g, mem_norm_g, mem_wkv, mem_qn, mem_kn, e_w_in, e_w_out, nsa_qn, nsa_kn, nsa_cmp_pe, nsa_cmp_w1, nsa_cmp_w2, mla_cqn, mla_wuq, mla_ckvn, mla_wuk, mla_wuv, mla_qn, mla_kn, o_w_in, o_w_out, moba_qn, moba_kn):
    xp, xs = x_prompt, x_sample
    depth = ln_g.shape[0]
    even_caches = (cache_nsa_cmp_k, cache_nsa_cmp_v, cache_nsa_sel_k, cache_nsa_sel_v,
                   state_nsa_win_k, state_nsa_win_v, cache_mla_ckv, cache_mla_krope)
    even_p, even_s, odd_p, odd_s, mem_k_p, mem_v_p = [], [], [], [], [], []
    for i in range(depth):
        km_p, vm_p = mem_kv(mem_prompt, mem_norm_g[i], mem_wkv[i], mem_kn[i])
        mem_k_p.append(km_p)
        mem_v_p.append(vm_p)
        li = i // 2
        if i % 2 == 0:
            W = {'w_in': e_w_in[li], 'w_out': e_w_out[li], 'nsa_qn': nsa_qn[li], 'nsa_kn': nsa_kn[li],
                 'cmp_pe': nsa_cmp_pe[li], 'cmp_w1': nsa_cmp_w1[li], 'cmp_w2': nsa_cmp_w2[li],
                 'mla_cqn': mla_cqn[li], 'mla_wuq': mla_wuq[li], 'mla_ckvn': mla_ckvn[li],
                 'mla_wuk': mla_wuk[li], 'mla_wuv': mla_wuv[li], 'mla_qn': mla_qn[li], 'mla_kn': mla_kn[li],
                 'mem_qn': mem_qn[i]}
            xp, st_p = even_prompt(xp, ln_g[i], W, km_p, vm_p, rel_bias)
            xs, st_s = even_sample(xs, ln_g[i], W, cache_mem_k[i], cache_mem_v[i], rel_bias, even_caches, li, page_table)
            even_p.append(st_p)
            even_s.append(st_s)
        else:
            W = {'w_in': o_w_in[li], 'w_out': o_w_out[li], 'moba_qn': moba_qn[li], 'moba_kn': moba_kn[li],
                 'mem_qn': mem_qn[i]}
            xp, st_p = odd_prompt(xp, ln_g[i], W, km_p, vm_p, rel_bias)
            xs, st_s = odd_sample(xs, ln_g[i], W, cache_mem_k[i], cache_mem_v[i], rel_bias,
                                  cache_moba_k, cache_moba_v, li, page_table)
            odd_p.append(st_p)
            odd_s.append(st_s)

    def stk(lst, j):
        return jnp.stack([t[j] for t in lst])

    return (xp, xs,
            stk(even_p, 0), stk(even_p, 1), stk(even_p, 2), stk(even_p, 3),
            stk(even_p, 4), stk(even_p, 5), stk(even_p, 6), stk(even_p, 7),
            stk(odd_p, 0), stk(odd_p, 1), jnp.stack(mem_k_p), jnp.stack(mem_v_p),
            stk(even_s, 0), stk(even_s, 1), stk(even_s, 2), stk(even_s, 3),
            stk(even_s, 4), stk(even_s, 5), stk(even_s, 6), stk(even_s, 7),
            stk(odd_s, 0), stk(odd_s, 1))
```

```python
import functools
import math

import jax
import jax.numpy as jnp
import numpy as np
from jax import lax
from jax.experimental import pallas as pl
from jax.experimental.pallas import tpu as pltpu

PAGE_SIZE = 128
HD = 64
N_BIAS_HEADS = 8
NSA_H = N_BIAS_HEADS
NSA_G = 2
NSA_HPG = NSA_H // NSA_G
CMP_STRIDE = 16
CMP_LEN = 2 * CMP_STRIDE
CMP_HID = 128
SEL_BLK = 64
SEL_TOPK = 16
NSA_WINDOW = 512
MLA_H = 4
Q_LORA = 256
KV_LORA = 128
NOPE_D = 64
ROPE_D = 32
QK_D = NOPE_D + ROPE_D
V_D = 64
ROPE_THETA = 10000.0
MOBA_H = N_BIAS_HEADS
MOBA_KVH = 2
MOBA_HPG = MOBA_H // MOBA_KVH
MOBA_BLK = 256
MOBA_TOPK = 3
MEM_H = 4
N_BUCKETS = 32
T5_MAX_DIST = 2048
Q_BLOCK = 128
EVEN_SIZES = (NSA_H * HD, 6 * NSA_G * HD, 3 * NSA_H, NSA_H * HD, Q_LORA, KV_LORA, ROPE_D, MLA_H * V_D, MEM_H * HD, MEM_H * HD)
ODD_SIZES = (MOBA_H * HD, 2 * MOBA_KVH * HD, MOBA_H * HD, MEM_H * HD, MEM_H * HD)
EPS = 1e-6
NEG = -1e30
FORCE = 1e9

VMEM_LIMIT_BYTES = 56 * 1024 * 1024
LANES = 128
ATT_TILE = 256
T5_SATURATED_DIST = 1536
assert T5_SATURATED_DIST > (N_BUCKETS // 2) * (T5_MAX_DIST / (N_BUCKETS // 2)) ** (15 / 16) + 16
T5_NEAR_TILES = 8
assert (T5_NEAR_TILES - 1) * ATT_TILE - (ATT_TILE - 1) >= T5_SATURATED_DIST


def _mm_kernel(*refs, has_gain, has_res):
    it = iter(refs)
    x_ref = next(it)
    g_ref = next(it) if has_gain else None
    w_ref = next(it)
    r_ref = next(it) if has_res else None
    o_ref = next(it)
    x = x_ref[...]
    if has_gain:
        x = x * lax.rsqrt(jnp.mean(x * x, axis=-1, keepdims=True) + EPS) * g_ref[...]
    acc = jnp.dot(x.astype(jnp.bfloat16), w_ref[...], preferred_element_type=jnp.float32)
    if has_res:
        acc = acc + r_ref[...]
    o_ref[...] = acc


def _matmul(x, w, gain=None, residual=None):
    M, K = x.shape
    N = w.shape[1]
    n_pad = -(-N // LANES) * LANES
    wb = jnp.pad(w, ((0, 0), (0, n_pad - N))).astype(jnp.bfloat16)
    tm = min(512, M)
    assert M % tm == 0 and tm % 8 == 0
    args = [x]
    specs = [pl.BlockSpec((tm, K), lambda i: (i, 0))]
    if gain is not None:
        args.append(gain.reshape(1, K).astype(jnp.float32))
        specs.append(pl.BlockSpec((1, K), lambda i: (0, 0)))
    args.append(wb)
    specs.append(pl.BlockSpec((K, n_pad), lambda i: (0, 0)))
    if residual is not None:
        assert n_pad == N
        args.append(residual)
        specs.append(pl.BlockSpec((tm, n_pad), lambda i: (i, 0)))
    out = pl.pallas_call(
        functools.partial(_mm_kernel, has_gain=gain is not None, has_res=residual is not None),
        grid=(M // tm,),
        in_specs=specs,
        out_specs=pl.BlockSpec((tm, n_pad), lambda i: (i, 0)),
        out_shape=jax.ShapeDtypeStruct((M, n_pad), jnp.float32),
        compiler_params=pltpu.CompilerParams(dimension_semantics=("arbitrary",), vmem_limit_bytes=VMEM_LIMIT_BYTES),
        name="rmsnorm_matmul",
    )(*args)
    return out[:, :N] if n_pad != N else out


def _flash_kernel(*refs, R, RK, RM, t, nk, nd, mode):
    it = iter(refs)
    q_ref, k_ref, v_ref = next(it), next(it), next(it)
    nm_ref = e_ref = bias_ref = None
    if RM:
        nm_ref, e_ref = next(it), next(it)
    if mode != 'full':
        bias_ref = next(it)
    o_ref, m_sc, l_sc, acc_sc = next(it), next(it), next(it), next(it)
    qt = pl.program_id(2)
    m_sc[...] = jnp.full(m_sc.shape, NEG, jnp.float32)
    l_sc[...] = jnp.zeros(l_sc.shape, jnp.float32)
    acc_sc[...] = jnp.zeros(acc_sc.shape, jnp.float32)
    q = q_ref[0, 0, 0]

    def step(kt, with_bias):
        ks = pl.ds(pl.multiple_of(kt * t, t), t)
        if RK == 1:
            s = jnp.dot(k_ref[0, 0, 0, ks, :], q, preferred_element_type=jnp.float32)
        else:
            s = jnp.concatenate(
                [jnp.dot(k_ref[0, 0, r, ks, :], q[:, r * t:(r + 1) * t], preferred_element_type=jnp.float32)
                 for r in range(R)], axis=1)
        if RM:
            mexp = jnp.dot(e_ref[kt], nm_ref[0, 0, 0], preferred_element_type=jnp.float32)
            s = s + (mexp if RM == R else jnp.tile(mexp, (1, R)))
        if with_bias:
            s = s + bias_ref[0, qt - kt]
        m_prev = m_sc[...]
        m_new = jnp.maximum(m_prev, jnp.max(s, axis=0, keepdims=True))
        alpha = jnp.exp(m_prev - m_new)
        p = jnp.exp(s - m_new)
        l_sc[...] = alpha * l_sc[...] + jnp.sum(p, axis=0, keepdims=True)
        acc_sc[...] = alpha * acc_sc[...] + jnp.dot(v_ref[0, 0, kt], p.astype(jnp.bfloat16),
                                                     preferred_element_type=jnp.float32)
        m_sc[...] = m_new

    def loop(lo, hi, with_bias):
        def body(kt, c):
            step(kt, with_bias)
            return c
        lax.fori_loop(lo, hi, body, 0)

    if mode == 'full':
        for kt in range(nk):
            step(kt, False)
    else:
        near_lo = jnp.maximum(qt - (nd - 1), 0)
        if mode == 'causal':
            loop(0, near_lo, False)
        loop(near_lo, qt + 1, True)
    o_ref[0, 0, 0] = acc_sc[...] / l_sc[...]


def _flash(q, k, v, negmask=None, expand=None, bias=None, *, mode, name):
    B, G, R, T, dk = q.shape
    RK, Tk = k.shape[2], k.shape[3]
    dv = v.shape[-1]
    t = min(ATT_TILE, T)
    assert T % t == 0 and Tk % t == 0 and (mode == 'full' or Tk == T)
    nq, nk = T // t, Tk // t
    RM = 0 if negmask is None else negmask.shape[2]
    nd = 0 if bias is None else bias.shape[1]
    q_t = jnp.transpose(q.reshape(B, G, R, nq, t, dk), (0, 1, 3, 5, 2, 4)).reshape(B, G, nq, dk, R * t)
    v_t = jnp.transpose(v.reshape(B, G, nk, t, dv), (0, 1, 2, 4, 3))
    args = [q_t, k, v_t]
    specs = [pl.BlockSpec((1, 1, 1, dk, R * t), lambda b, g, i: (b, g, i, 0, 0)),
             pl.BlockSpec((1, 1, RK, Tk, dk), lambda b, g, i: (b, g, 0, 0, 0)),
             pl.BlockSpec((1, 1, nk, dv, t), lambda b, g, i: (b, g, 0, 0, 0))]
    if RM:
        nb = negmask.shape[-1]
        nm_t = jnp.transpose(negmask.reshape(B, G, RM, nq, t, nb), (0, 1, 3, 5, 2, 4)).reshape(B, G, nq, nb, RM * t)
        e_t = jnp.swapaxes(expand, 1, 2)
        args += [nm_t, e_t]
        specs += [pl.BlockSpec((1, 1, 1, nb, RM * t), lambda b, g, i: (b, g, i, 0, 0)),
                  pl.BlockSpec(e_t.shape, lambda b, g, i: (0, 0, 0))]
    if mode != 'full':
        args.append(jnp.swapaxes(bias, 2, 3))
        specs.append(pl.BlockSpec((1, nd, t, R * t), lambda b, g, i: (g, 0, 0, 0)))
    o_t = pl.pallas_call(
        functools.partial(_flash_kernel, R=R, RK=RK, RM=RM, t=t, nk=nk, nd=nd, mode=mode),
        grid=(B, G, nq),
        in_specs=specs,
        out_specs=pl.BlockSpec((1, 1, 1, dv, R * t), lambda b, g, i: (b, g, i, 0, 0)),
        out_shape=jax.ShapeDtypeStruct((B, G, nq, dv, R * t), jnp.float32),
        scratch_shapes=[pltpu.VMEM((1, R * t), jnp.float32), pltpu.VMEM((1, R * t), jnp.float32),
                        pltpu.VMEM((dv, R * t), jnp.float32)],
        compiler_params=pltpu.CompilerParams(dimension_semantics=("arbitrary", "arbitrary", "arbitrary"),
                                             vmem_limit_bytes=VMEM_LIMIT_BYTES),
        name=name,
    )(*args)
    return jnp.transpose(o_t.reshape(B, G, nq, dv, R, t), (0, 1, 4, 2, 5, 3)).reshape(B, G, R, T, dv)


def _page_specs(n_per_step, n_pages, li, block):
    def index_map(b, c, pt_ref, *, j):
        return (li, pt_ref[b * n_pages + c * n_per_step + j]) + (0,) * (len(block) - 2)
    return [pl.BlockSpec(block, functools.partial(index_map, j=j)) for j in range(n_per_step)]


def _pages_per_step(n_pages):
    pp = 16 if n_pages % 16 == 0 else 8
    assert n_pages % pp == 0
    return pp


_NT = (((1,), (1,)), ((), ()))


def _softmax_update(s, v, m_sc, l_sc, acc_sc, values_transposed=False):
    m_prev = m_sc[...]
    m_new = jnp.maximum(m_prev, jnp.max(s, axis=-1, keepdims=True))
    alpha = jnp.exp(m_prev - m_new)
    p = jnp.exp(s - m_new)
    l_sc[...] = alpha * l_sc[...] + jnp.sum(p, axis=-1, keepdims=True)
    if values_transposed:
        pv = lax.dot_general(p.astype(jnp.bfloat16), v, _NT, preferred_element_type=jnp.float32)
    else:
        pv = jnp.dot(p.astype(jnp.bfloat16), v, preferred_element_type=jnp.float32)
    acc_sc[...] = alpha * acc_sc[...] + pv
    m_sc[...] = m_new


def _paged_attn_kernel(pt_ref, q_ref, *refs, pp, nch, nbc):
    k_refs, v_refs = refs[:pp], refs[pp:2 * pp]
    nm_ref, e_ref, bias_ref, knew_ref, vnew_ref, bnew_ref, o_ref, m_sc, l_sc, acc_sc = refs[2 * pp:]
    c = pl.program_id(1)

    @pl.when(c == 0)
    def _():
        m_sc[...] = jnp.full(m_sc.shape, NEG, jnp.float32)
        l_sc[...] = jnp.zeros(l_sc.shape, jnp.float32)
        acc_sc[...] = jnp.zeros(acc_sc.shape, jnp.float32)

    q = q_ref[0]
    k = jnp.concatenate([r[0, 0] for r in k_refs], axis=1).astype(jnp.bfloat16)
    v = jnp.concatenate([r[0, 0] for r in v_refs], axis=1).astype(jnp.bfloat16)
    s = jnp.dot(q, k, preferred_element_type=jnp.float32)
    s = s + jnp.dot(nm_ref[0], e_ref[c], preferred_element_type=jnp.float32)
    near = (c >= nch - nbc).astype(jnp.float32)
    s = s + near * bias_ref[0]
    _softmax_update(s, v, m_sc, l_sc, acc_sc, values_transposed=True)

    @pl.when(c == nch - 1)
    def _():
        s_new = jnp.dot(q, knew_ref[0], preferred_element_type=jnp.float32) + bnew_ref[...]
        _softmax_update(s_new, vnew_ref[0], m_sc, l_sc, acc_sc, values_transposed=True)
        o_ref[0] = acc_sc[...] / l_sc[...]


def _paged_attn(q_bd, pool_k, pool_v, li, pt_flat, n_pages, negmask, expand, bias_last, k_new, v_new, bias_new, *, name):
    B, RQ, W = q_bd.shape
    pp = _pages_per_step(n_pages)
    nch = n_pages // pp
    nbc = bias_last.shape[0]
    nb = negmask.shape[-1]
    page = (1, 1, W, PAGE_SIZE)
    in_specs = ([pl.BlockSpec((1, RQ, W), lambda b, c, pt: (b, 0, 0))]
                + _page_specs(pp, n_pages, li, page) + _page_specs(pp, n_pages, li, page)
                + [pl.BlockSpec((1, RQ, nb), lambda b, c, pt: (b, 0, 0)),
                   pl.BlockSpec(expand.shape, lambda b, c, pt: (0, 0, 0)),
                   pl.BlockSpec((1, RQ, pp * PAGE_SIZE), lambda b, c, pt: (jnp.maximum(c - (nch - nbc), 0), 0, 0)),
                   pl.BlockSpec((1, W, PAGE_SIZE), lambda b, c, pt: (b, 0, 0)),
                   pl.BlockSpec((1, W, PAGE_SIZE), lambda b, c, pt: (b, 0, 0)),
                   pl.BlockSpec((RQ, PAGE_SIZE), lambda b, c, pt: (0, 0))])
    return pl.pallas_call(
        functools.partial(_paged_attn_kernel, pp=pp, nch=nch, nbc=nbc),
        grid_spec=pltpu.PrefetchScalarGridSpec(
            num_scalar_prefetch=1, grid=(B, nch), in_specs=in_specs,
            out_specs=pl.BlockSpec((1, RQ, W), lambda b, c, pt: (b, 0, 0)),
            scratch_shapes=[pltpu.VMEM((RQ, 1), jnp.float32), pltpu.VMEM((RQ, 1), jnp.float32),
                            pltpu.VMEM((RQ, W), jnp.float32)]),
        out_shape=jax.ShapeDtypeStruct((B, RQ, W), jnp.float32),
        compiler_params=pltpu.CompilerParams(dimension_semantics=("arbitrary", "arbitrary"),
                                             vmem_limit_bytes=VMEM_LIMIT_BYTES),
        name=name,
    )(pt_flat, q_bd, *([pool_k] * pp), *([pool_v] * pp), negmask, expand, bias_last, k_new, v_new, bias_new)


def _paged_mla_kernel(pt_ref, qn_ref, qr_ref, *refs, pp, nch, n_heads, n_q):
    c_refs, r_refs = refs[:pp], refs[pp:2 * pp]
    (wuk_ref, grope_ref, hsel_ref, cos_ref, sin_ref, cnew_ref, rnew_ref, cosn_ref, sinn_ref, bnew_ref,
     o_ref, m_sc, l_sc, acc_sc) = refs[2 * pp:]
    c = pl.program_id(1)

    @pl.when(c == 0)
    def _():
        m_sc[...] = jnp.full(m_sc.shape, NEG, jnp.float32)
        l_sc[...] = jnp.zeros(l_sc.shape, jnp.float32)
        acc_sc[...] = jnp.zeros(acc_sc.shape, jnp.float32)

    def head_sums(x, sel):
        return lax.dot_general(sel, x.astype(jnp.bfloat16), _NT, preferred_element_type=jnp.float32)

    def scores(c32, kr_t, cos, sin):
        cb = c32.astype(jnp.bfloat16)
        kn = jnp.dot(cb, wuk_ref[...], preferred_element_type=jnp.float32)
        ss = head_sums(kn * kn, hsel_ref[...]) + jnp.sum(kr_t * kr_t, axis=0, keepdims=True)
        inv = lax.rsqrt(ss / QK_D + EPS) * QK_D ** -0.5
        x = kr_t * grope_ref[...]
        half = ROPE_D // 2
        x1, x2 = x[:half], x[half:]
        krr = jnp.concatenate([x1 * cos - x2 * sin, x2 * cos + x1 * sin], axis=0).astype(jnp.bfloat16)
        s = (lax.dot_general(qn_ref[0], kn.astype(jnp.bfloat16), _NT, preferred_element_type=jnp.float32)
             + jnp.dot(qr_ref[0], krr, preferred_element_type=jnp.float32))
        n = s.shape[1]
        inv_rows = jnp.concatenate([jnp.broadcast_to(inv[h:h + 1], (n_q, n)) for h in range(n_heads)], axis=0)
        return s * inv_rows, cb

    c32 = jnp.concatenate([r[0, 0] for r in c_refs], axis=0)
    kr32 = jnp.concatenate([r[0, 0] for r in r_refs], axis=1)
    s, cb = scores(c32, kr32, cos_ref[0], sin_ref[0])
    _softmax_update(s, cb, m_sc, l_sc, acc_sc)

    @pl.when(c == nch - 1)
    def _():
        s_new, cb_new = scores(cnew_ref[0], rnew_ref[0], cosn_ref[...], sinn_ref[...])
        _softmax_update(s_new + bnew_ref[...], cb_new, m_sc, l_sc, acc_sc)
        o_ref[0] = acc_sc[...] / l_sc[...]


def _paged_mla(qn_bd, qr, pool_c, pool_r, li, pt_flat, n_pages, wuk, grope, cos, sin, c_new, r_new, cos_new, sin_new,
               bias_new, n_heads):
    B, RQ, _ = qn_bd.shape
    pp = _pages_per_step(n_pages)
    nch = n_pages // pp
    half = ROPE_D // 2
    hsel = (jnp.arange(8)[:, None] == (jnp.arange(n_heads * NOPE_D) // NOPE_D)[None, :]).astype(jnp.bfloat16)
    const = lambda shape: pl.BlockSpec(shape, lambda b, c, pt: (0,) * len(shape))
    per_b = lambda shape: pl.BlockSpec(shape, lambda b, c, pt: (b,) + (0,) * (len(shape) - 1))
    in_specs = ([per_b((1, RQ, n_heads * NOPE_D)), per_b((1, RQ, ROPE_D))]
                + _page_specs(pp, n_pages, li, (1, 1, PAGE_SIZE, KV_LORA))
                + _page_specs(pp, n_pages, li, (1, 1, ROPE_D, PAGE_SIZE))
                + [const(wuk.shape), const((ROPE_D, 1)), const(hsel.shape),
                   pl.BlockSpec((1, half, pp * PAGE_SIZE), lambda b, c, pt: (c, 0, 0)),
                   pl.BlockSpec((1, half, pp * PAGE_SIZE), lambda b, c, pt: (c, 0, 0)),
                   per_b((1, PAGE_SIZE, KV_LORA)), per_b((1, ROPE_D, PAGE_SIZE)),
                   const((half, PAGE_SIZE)), const((half, PAGE_SIZE)), const((RQ, PAGE_SIZE))])
    return pl.pallas_call(
        functools.partial(_paged_mla_kernel, pp=pp, nch=nch, n_heads=n_heads, n_q=RQ // n_heads),
        grid_spec=pltpu.PrefetchScalarGridSpec(
            num_scalar_prefetch=1, grid=(B, nch), in_specs=in_specs,
            out_specs=per_b((1, RQ, KV_LORA)),
            scratch_shapes=[pltpu.VMEM((RQ, 1), jnp.float32), pltpu.VMEM((RQ, 1), jnp.float32),
                            pltpu.VMEM((RQ, KV_LORA), jnp.float32)]),
        out_shape=jax.ShapeDtypeStruct((B, RQ, KV_LORA), jnp.float32),
        compiler_params=pltpu.CompilerParams(dimension_semantics=("arbitrary", "arbitrary"),
                                             vmem_limit_bytes=VMEM_LIMIT_BYTES),
        name="mla_sample",
    )(pt_flat, qn_bd, qr, *([pool_c] * pp), *([pool_r] * pp), wuk, grope, hsel, cos, sin, c_new, r_new,
      cos_new, sin_new, bias_new)


def _paged_cmp_proj_kernel(pt_ref, *refs, pp, stride):
    w_ref, o_ref, x_sc = refs[pp:]
    page = x_sc.shape[0] // pp
    for j in range(pp):
        x_sc[j * page:(j + 1) * page, :] = refs[j][0, 0].T
    chunks = x_sc.shape[0] // stride
    flat = jnp.concatenate([x_sc[pl.ds(s, chunks, stride=stride), :].astype(jnp.bfloat16) for s in range(stride)], axis=1)
    o_ref[0] = jnp.dot(flat, w_ref[...], preferred_element_type=jnp.float32)


def _paged_cmp_proj(pool, li, pt_flat, B, n_pages, w_rows):
    pp = _pages_per_step(n_pages)
    stride, W, n_out = w_rows.shape
    cpp = PAGE_SIZE // stride
    w_rows = w_rows.reshape(stride * W, n_out)
    return pl.pallas_call(
        functools.partial(_paged_cmp_proj_kernel, pp=pp, stride=stride),
        grid_spec=pltpu.PrefetchScalarGridSpec(
            num_scalar_prefetch=1, grid=(B, n_pages // pp),
            in_specs=_page_specs(pp, n_pages, li, (1, 1, W, PAGE_SIZE))
            + [pl.BlockSpec(w_rows.shape, lambda b, c, pt: (0, 0))],
            out_specs=pl.BlockSpec((1, pp * cpp, n_out), lambda b, c, pt: (b, c, 0)),
            scratch_shapes=[pltpu.VMEM((pp * PAGE_SIZE, W), jnp.float32)]),
        out_shape=jax.ShapeDtypeStruct((B, n_pages * cpp, n_out), jnp.float32),
        compiler_params=pltpu.CompilerParams(dimension_semantics=("arbitrary", "arbitrary"),
                                             vmem_limit_bytes=VMEM_LIMIT_BYTES),
        name="nsa_compress_sample",
    )(pt_flat, *([pool] * pp), w_rows)


def _paged_sum_kernel(pt_ref, *refs, pp):
    o_ref = refs[pp]
    o_ref[0, 0] = jnp.concatenate([jnp.sum(r[0, 0], axis=1, keepdims=True) for r in refs[:pp]], axis=1)


def _paged_sum(pool, li, pt_flat, B, n_pages):
    pp = _pages_per_step(n_pages)
    W = pool.shape[2]
    out = pl.pallas_call(
        functools.partial(_paged_sum_kernel, pp=pp),
        grid_spec=pltpu.PrefetchScalarGridSpec(
            num_scalar_prefetch=1, grid=(B, n_pages // pp),
            in_specs=_page_specs(pp, n_pages, li, (1, 1, W, PAGE_SIZE)),
            out_specs=pl.BlockSpec((1, 1, W, pp), lambda b, c, pt: (b, c, 0, 0))),
        out_shape=jax.ShapeDtypeStruct((B, n_pages // pp, W, pp), jnp.float32),
        compiler_params=pltpu.CompilerParams(dimension_semantics=("arbitrary", "arbitrary"),
                                             vmem_limit_bytes=VMEM_LIMIT_BYTES),
        name="moba_page_sums",
    )(pt_flat, *([pool] * pp))
    return jnp.transpose(out, (0, 1, 3, 2)).reshape(B, n_pages, W)


def rms_norm(x, g):
    xf = x.astype(jnp.float32)
    y = xf * lax.rsqrt(jnp.mean(xf * xf, axis=-1, keepdims=True) + EPS)
    return (y * g.astype(jnp.float32)).astype(x.dtype)


def _project_split(x2d, w, sizes, gain):
    starts = np.cumsum((0,) + tuple(sizes))[:-1]
    padded = [-(-n // LANES) * LANES for n in sizes]
    w_al = jnp.concatenate([jnp.pad(w[:, o:o + n], ((0, 0), (0, p - n))) for o, n, p in zip(starts, sizes, padded)], axis=1)
    h = _matmul(x2d, w_al, gain=gain)
    al = np.cumsum([0] + padded)[:-1]
    return [h[:, o:o + n] for o, n in zip(al, sizes)]


def masked_softmax(s, mask):
    p = jax.nn.softmax(jnp.where(mask, s, NEG), axis=-1)
    return p * mask


def t5_bucket(dist):
    dist = jnp.maximum(dist, 0)
    exact = N_BUCKETS // 2
    far = exact + (jnp.log(jnp.maximum(dist, 1).astype(jnp.float32) / exact)
                   / math.log(T5_MAX_DIST / exact) * (N_BUCKETS - exact)).astype(jnp.int32)
    return jnp.where(dist < exact, dist, jnp.minimum(far, N_BUCKETS - 1))


def rope(x, pos):
    half = ROPE_D // 2
    inv = ROPE_THETA ** (-jnp.arange(half, dtype=jnp.float32) / half)
    ang = pos.astype(jnp.float32)[:, None] * inv
    ang = ang.reshape((ang.shape[0],) + (1,) * (x.ndim - 3) + (half,))
    cos, sin = jnp.cos(ang), jnp.sin(ang)
    xf = x.astype(jnp.float32)
    x1, x2 = xf[..., :half], xf[..., half:]
    return jnp.concatenate([x1 * cos - x2 * sin, x2 * cos + x1 * sin], axis=-1).astype(x.dtype)


def _bias_lookup(rel_bias, bucket):
    onehot = (bucket[..., None] == jnp.arange(N_BUCKETS)).astype(jnp.float32)
    return jnp.einsum('...k,kh->...h', onehot, rel_bias.astype(jnp.float32), precision=lax.Precision.HIGHEST)


def _sample_bias(rel_bias, S, past, keys_per_chunk, nbc):
    H = rel_bias.shape[1]
    rel = rel_bias - rel_bias[N_BUCKETS - 1]
    n = nbc * keys_per_chunk
    s = jnp.arange(S)
    dist = (past + s)[:, None] - (past - n + jnp.arange(n))[None, :]
    b = jnp.transpose(_bias_lookup(rel, t5_bucket(dist)), (2, 0, 1)).reshape(H * S, nbc, keys_per_chunk)
    j = jnp.arange(PAGE_SIZE)
    dn = s[:, None] - j[None, :]
    ok = (dn >= 0) & (j < S)[None, :]
    bn = jnp.where(ok[..., None], _bias_lookup(rel, t5_bucket(dn)), NEG)
    return jnp.transpose(b, (1, 0, 2)), jnp.transpose(bn, (2, 0, 1)).reshape(H * S, PAGE_SIZE)


def _transposed_pages(pool):
    L, n_pool, page = pool.shape[:3]
    return jnp.moveaxis(pool.reshape(L, n_pool, page, -1), 2, 3)


def _block_diag_rows(q, groups):
    B, S, H, D = q.shape
    qt = jnp.transpose(q, (0, 2, 1, 3)).reshape(B, groups, (H // groups) * S, D)
    out = qt[:, :, :, None, :] * jnp.eye(groups, dtype=q.dtype)[None, :, None, :, None]
    return out.reshape(B, H * S, groups * D)


def _block_diag_extract(o, groups, S):
    B, RQ, W = o.shape
    D = W // groups
    o5 = o.reshape(B, groups, RQ // groups, groups, D)
    od = jnp.stack([o5[:, g, :, g, :] for g in range(groups)], axis=1)
    return jnp.transpose(od.reshape(B, RQ // S, S, D), (0, 2, 1, 3))


def _bias_tiles(rel_bias, heads_per_group, t, nd, window=None):
    i = jnp.arange(t)[:, None]
    j = jnp.arange(t)[None, :]
    dist = jnp.arange(nd)[:, None, None] * t + i - j
    b = _bias_lookup(rel_bias - rel_bias[N_BUCKETS - 1], t5_bucket(dist))
    ok = dist >= 0
    if window is not None:
        ok = ok & (dist < window)
    b = jnp.where(ok[..., None], b, NEG)
    n_heads = rel_bias.shape[1]
    b = jnp.transpose(b, (3, 0, 1, 2)).reshape(n_heads // heads_per_group, heads_per_group, nd, t, t)
    return jnp.transpose(b, (0, 2, 1, 3, 4)).reshape(n_heads // heads_per_group, nd, heads_per_group * t, t).astype(jnp.float32)


def _causal_tiles(n_groups, rows, t):
    ok = jnp.arange(t)[:, None] >= jnp.arange(t)[None, :]
    b = jnp.where(ok, 0.0, NEG).astype(jnp.float32)
    return jnp.broadcast_to(jnp.tile(b, (rows, 1))[None, None], (n_groups, 1, rows * t, t))


def _expand_onehot(n_keys, blk, nb, t):
    key_blk = (jnp.arange(n_keys) // blk).reshape(n_keys // t, 1, t)
    return (key_blk == jnp.arange(nb)[None, :, None]).astype(jnp.bfloat16)


def _heads_first(x, groups):
    B, T, H, D = x.shape
    return jnp.transpose(x.reshape(B, T, groups, H // groups, D), (0, 2, 3, 1, 4))


def _heads_last(o):
    B, G, R, T, D = o.shape
    return jnp.transpose(o, (0, 3, 1, 2, 4)).reshape(B, T, G * R, D)


def cmp_chunk_proj(raw, w1):
    B, L = raw.shape[:2]
    ch = raw.reshape(B, L // CMP_STRIDE, CMP_STRIDE, NSA_G, HD)
    a = jnp.einsum('bnsgd,sdh->bngh', ch, w1[:CMP_STRIDE])
    b = jnp.einsum('bnsgd,sdh->bngh', ch, w1[CMP_STRIDE:])
    return a, b


def cmp_finish(a, b, pe, w1, w2):
    pe_h = jnp.einsum('sd,sdh->h', pe, w1)
    h = jax.nn.gelu(a[:, :-1] + b[:, 1:] + pe_h)
    return jnp.einsum('bngh,hd->bngd', h, w2)


def nsa_compress(raw_k_parts, raw_v_parts, W):
    out = []
    for j, parts in enumerate((raw_k_parts, raw_v_parts)):
        w1 = W['cmp_w1'][j]
        ab = [cmp_chunk_proj(r, w1) for r in parts]
        a = jnp.concatenate([t[0] for t in ab], axis=1)
        b = jnp.concatenate([t[1] for t in ab], axis=1)
        out.append(cmp_finish(a, b, W['cmp_pe'][j], w1, W['cmp_w2'][j]))
    return rms_norm(out[0], W['nsa_kn'][0]), out[1]


def sel_block_scores(imp, nsb):
    r = SEL_BLK // CMP_STRIDE
    front = CMP_LEN // CMP_STRIDE - 1
    nc = imp.shape[-1]
    imp = jnp.pad(imp, [(0, 0)] * (imp.ndim - 1) + [(front, r * nsb - nc)])
    score = 0.0
    for o in range(-front, r):
        lo = o * CMP_STRIDE
        w = max(0, min(lo + CMP_LEN, SEL_BLK) - max(lo, 0)) / CMP_LEN
        start = o + front
        score = score + w * imp[..., start:start + r * (nsb - 1) + 1:r]
    return score


def _topk_mask(score, k):
    iota = jnp.arange(score.shape[-1])
    mask = jnp.zeros(score.shape, bool)
    for _ in range(k):
        hit = iota == jnp.argmax(score, axis=-1)[..., None]
        mask = mask | hit
        score = jnp.where(hit, -jnp.inf, score)
    return mask


def _nsa_compressed_and_select(q, q_pos, kc, vc, nsb, bias_t):
    B, Q = q.shape[:2]
    scale = HD ** -0.5
    f32 = jnp.float32
    qg = q.reshape(B, Q, NSA_G, NSA_HPG, HD)
    nc = kc.shape[1]
    dist_c = q_pos[:, None] - (jnp.arange(nc) * CMP_STRIDE + CMP_LEN - 1)[None, :]
    bias_c = _bias_lookup(bias_t, t5_bucket(dist_c)).reshape(Q, nc, NSA_G, NSA_HPG)
    bias_c = jnp.transpose(bias_c, (0, 2, 3, 1))[None]
    s_c =jnp.einsum('bqgpd,bngd->bqgpn', qg, kc, preferred_element_type=f32) * scale + bias_c
    p_c = masked_softmax(s_c, (dist_c >= 0)[None, :, None, None, :])
    o_c = jnp.einsum('bqgpn,bngd->bqgpd', p_c.astype(vc.dtype), vc)
    score = sel_block_scores(p_c.sum(axis=3), nsb)
    blk = jnp.arange(nsb)[None, :]
    own = (q_pos // SEL_BLK)[:, None]
    forced = (blk == 0) | (blk == own) | (blk == own - 1)
    score = jnp.where(forced[None, :, None, :], FORCE,
                      jnp.where((blk <= own)[None, :, None, :], score, -jnp.inf))
    sel = _topk_mask(score, min(SEL_TOPK, nsb)) & (blk <= own)[None, :, None, :]
    return o_c, sel


def _nsa_window_sample(q, q_pos, kw, vw, kw_pos, bias_t):
    B, Q = q.shape[:2]
    qg = q.reshape(B, Q, NSA_G, NSA_HPG, HD)
    dist_w = q_pos[:, None] - kw_pos[None, :]
    mask_w = (dist_w >= 0) & (dist_w < NSA_WINDOW) & (kw_pos >= 0)[None, :]
    bias_w = _bias_lookup(bias_t, t5_bucket(dist_w)).reshape(Q, kw_pos.shape[0], NSA_G, NSA_HPG)
    bias_w = jnp.transpose(bias_w, (0, 2, 3, 1))[None]
    s_w = jnp.einsum('bqgpd,blgd->bqgpl', qg, kw, preferred_element_type=jnp.float32) * HD ** -0.5 + bias_w
    p_w = masked_softmax(s_w, mask_w[None, :, None, None, :])
    return jnp.einsum('bqgpl,blgd->bqgpd', p_w.astype(vw.dtype), vw)


def _gqa_sample_attention(q, sel_rows, pool_k, pool_v, li, pt_flat, n_pages, blk, k_new, v_new, bias_t, groups, name):
    B, S, H, D = q.shape
    L, n_pool = pool_k.shape[:2]
    past = n_pages * PAGE_SIZE
    assert past % blk == 0 and S <= blk and S <= PAGE_SIZE
    pp = _pages_per_step(n_pages)
    keys = pp * PAGE_SIZE
    nbc = min(n_pages // pp, -(-T5_SATURATED_DIST // keys))
    W = groups * D
    q_bd = _block_diag_rows(q * D ** -0.5, groups).astype(jnp.bfloat16)
    negmask = jnp.where(sel_rows, 0.0, NEG).astype(jnp.bfloat16)
    expand = _expand_onehot(past, blk, past // blk, keys)
    bias_last, bias_new = _sample_bias(bias_t, S, past, keys, nbc)
    padr = ((0, 0), (0, PAGE_SIZE - S), (0, 0))
    kn = jnp.swapaxes(jnp.pad(k_new.reshape(B, S, W), padr), 1, 2).astype(jnp.bfloat16)
    vn = jnp.swapaxes(jnp.pad(v_new.reshape(B, S, W), padr), 1, 2).astype(jnp.bfloat16)
    o = _paged_attn(q_bd, _transposed_pages(pool_k), _transposed_pages(pool_v), li, pt_flat,
                    n_pages, negmask, expand, bias_last, kn, vn, bias_new, name=name)
    return _block_diag_extract(o, groups, S)


def _nsa_compress_sample(pools, li, pt_flat, B, n_pages, new_parts, W):
    out = []
    chunks_per_page = PAGE_SIZE // CMP_STRIDE
    for j, (pool, new) in enumerate(zip(pools, new_parts)):
        w1 = W['cmp_w1'][j]
        w1r = w1.reshape(2, CMP_STRIDE, HD, CMP_HID)
        w_rows = jnp.einsum('wsdh,gk->sgdwkh', w1r, jnp.eye(NSA_G, dtype=w1.dtype))
        w_rows = w_rows.reshape(CMP_STRIDE, NSA_G * HD, 2 * NSA_G * CMP_HID).astype(jnp.bfloat16)
        ab = _paged_cmp_proj(_transposed_pages(pool), li, pt_flat, B, n_pages, w_rows)
        ab = ab.reshape(B, n_pages * chunks_per_page, 2, NSA_G, CMP_HID)
        a_new, b_new = cmp_chunk_proj(new, w1)
        a = jnp.concatenate([ab[:, :, 0], a_new], axis=1)
        b = jnp.concatenate([ab[:, :, 1], b_new], axis=1)
        out.append(cmp_finish(a, b, W['cmp_pe'][j], w1, W['cmp_w2'][j]))
    return rms_norm(out[0], W['nsa_kn'][0]), out[1]


def mla_keys(ckv, kr, k_pos, W):
    kn = jnp.einsum('blc,chd->blhd', ckv, W['mla_wuk'])
    krf = kr.astype(jnp.float32)
    ss = jnp.einsum('blhd,blhd->blh', kn, kn, preferred_element_type=jnp.float32) + jnp.sum(krf * krf, -1)[..., None]
    inv = lax.rsqrt(ss / QK_D + EPS)
    krr = rope(kr * W['mla_kn'][NOPE_D:], k_pos)
    return kn, inv, krr


def _rope_tables(pos):
    half = ROPE_D // 2
    inv = ROPE_THETA ** (-jnp.arange(half, dtype=jnp.float32) / half)
    ang = pos.astype(jnp.float32)[:, None] * inv
    return jnp.cos(ang), jnp.sin(ang)


def _mla_sample(P, c_ckv, c_kr, li, pt_flat, n_pages, W):
    B, S = P['ckv'].shape[:2]
    assert S <= PAGE_SIZE
    past = n_pages * PAGE_SIZE
    pp = _pages_per_step(n_pages)
    qn = _block_diag_rows(P['q_nope'] * W['mla_kn'][:NOPE_D], MLA_H).astype(jnp.bfloat16)
    qr = jnp.transpose(P['q_rope'], (0, 2, 1, 3)).reshape(B, MLA_H * S, ROPE_D).astype(jnp.bfloat16)
    chunked_t = lambda a: jnp.swapaxes(a.reshape(n_pages // pp, pp * PAGE_SIZE, -1), 1, 2)
    cos, sin = [chunked_t(a) for a in _rope_tables(jnp.arange(past))]
    cos_new, sin_new = [a.T for a in _rope_tables(past + jnp.arange(PAGE_SIZE))]
    padr = ((0, 0), (0, PAGE_SIZE - S), (0, 0))
    j = jnp.arange(PAGE_SIZE)[None, :]
    s = jnp.arange(S)[:, None]
    bias_new = jnp.tile(jnp.where((j <= s) & (j < S), 0.0, NEG).astype(jnp.float32), (MLA_H, 1))
    o_lat = _paged_mla(qn, qr, c_ckv, _transposed_pages(c_kr), li, pt_flat, n_pages,
                       W['mla_wuk'].reshape(KV_LORA, MLA_H * NOPE_D).astype(jnp.bfloat16),
                       W['mla_kn'][NOPE_D:].reshape(ROPE_D, 1).astype(jnp.float32), cos, sin,
                       jnp.pad(P['ckv'], padr), jnp.swapaxes(jnp.pad(P['kr'], padr), 1, 2), cos_new, sin_new,
                       bias_new, MLA_H)
    o_lat = jnp.transpose(o_lat.reshape(B, MLA_H, S, KV_LORA), (0, 2, 1, 3))
    return jnp.einsum('bqhc,chd->bqhd', o_lat, W['mla_wuv']).reshape(B, S, MLA_H * V_D)


def _mla_prompt(P, pos, W):
    kn, inv, krr = mla_keys(P['ckv'], P['kr'], pos, W)
    B, T = kn.shape[:2]
    kfull = jnp.concatenate([kn, jnp.broadcast_to(krr[:, :, None, :], (B, T, MLA_H, ROPE_D))], axis=-1)
    kfull = kfull * (inv * QK_D ** -0.5)[..., None]
    qfull = jnp.concatenate([P['q_nope'] * W['mla_kn'][:NOPE_D], P['q_rope']], axis=-1)
    padd = ((0, 0), (0, 0), (0, 0), (0, LANES - QK_D))
    kk = _heads_first(jnp.pad(kfull, padd), 1).astype(jnp.bfloat16)
    qq = _heads_first(jnp.pad(qfull, padd), 1).astype(jnp.bfloat16)
    vv = P['ckv'].astype(jnp.bfloat16)[:, None]
    t = min(ATT_TILE, T)
    o_lat = _flash(qq, kk, vv, bias=_causal_tiles(1, MLA_H, t), mode='causal', name="mla_prompt")
    o_lat = _heads_last(o_lat)
    return jnp.einsum('bqhc,chd->bqhd', o_lat, W['mla_wuv']).reshape(B, T, MLA_H * V_D)


def moba_means_paged(page_sum, n_pages, k_new):
    B = page_sum.shape[0]
    past = n_pages * PAGE_SIZE
    total = past + k_new.shape[1]
    nbm = -(-total // MOBA_BLK)
    pps = MOBA_BLK // PAGE_SIZE
    npp = -(-n_pages // pps) * pps
    page_sum = jnp.pad(page_sum, ((0, 0), (0, npp - n_pages), (0, 0), (0, 0)))
    blk_sum = page_sum.reshape(B, npp // pps, pps, MOBA_KVH, HD).sum(axis=2)
    blk_sum = jnp.pad(blk_sum, ((0, 0), (0, nbm - npp // pps), (0, 0), (0, 0)))
    new_blk = (past + jnp.arange(k_new.shape[1])) // MOBA_BLK
    onehot = (new_blk[:, None] == jnp.arange(nbm)[None, :]).astype(jnp.float32)
    blk_sum = blk_sum + jnp.einsum('bsgd,sn->bngd', k_new.astype(jnp.float32), onehot)
    return blk_sum / MOBA_BLK


def _moba_pick_blocks(q, q_pos, kmean_h):
    nbm = kmean_h.shape[1]
    own = q_pos // MOBA_BLK
    blk = jnp.arange(nbm)
    full_past = (blk[None, :] < own[:, None])[None, :, None, :]
    k_eff = min(MOBA_TOPK, nbm - 1)
    if k_eff <= 0:
        return jnp.zeros(q.shape[:3] + (nbm,), bool)
    gs = jnp.einsum('bqhd,bnhd->bqhn', q, kmean_h, preferred_element_type=jnp.float32)
    gs = jnp.where(full_past, gs, -jnp.inf)
    return _topk_mask(gs, k_eff) & full_past


def mem_kv(mem, g, wkv, kn):
    B, N = mem.shape[:2]
    h = _matmul(mem.reshape(B * N, -1), wkv, gain=g).reshape(B, N, 2, MEM_H, HD)
    return rms_norm(h[:, :, 0], kn), h[:, :, 1]


def mem_attend(q, km, vm):
    B, Q = q.shape[:2]
    s = jnp.einsum('bqhd,bmhd->bhqm', q, km, preferred_element_type=jnp.float32) * HD ** -0.5
    p = jax.nn.softmax(s, axis=-1).astype(vm.dtype)
    return jnp.einsum('bhqm,bmhd->bqhd', p, vm).reshape(B, Q, MEM_H * HD)


def _mem_attend_prompt(q, km, vm):
    B, T = q.shape[:2]
    qq = _heads_first(q * HD ** -0.5, MEM_H).astype(jnp.bfloat16)
    kk = _heads_first(km, MEM_H).astype(jnp.bfloat16)
    vv = _heads_first(vm, MEM_H)[:, :, 0].astype(jnp.bfloat16)
    return _heads_last(_flash(qq, kk, vv, mode='full', name="mem_prompt")).reshape(B, T, MEM_H * HD)


def even_project(x, ln, W, pos):
    B, T = x.shape[:2]
    pieces = _project_split(x.reshape(B * T, -1), W['w_in'], EVEN_SIZES, ln)
    nq, nkv, ngt, nz, cq, ckv, kr, mz, mq, memz = [p.reshape(B, T, -1) for p in pieces]
    kv = nkv.reshape(B, T, 6, NSA_G, HD)
    qm = _matmul(cq.reshape(B * T, -1), W['mla_wuq'], gain=W['mla_cqn']).reshape(B, T, MLA_H, QK_D)
    qm = rms_norm(qm, W['mla_qn'])
    return {
        'q': rms_norm(nq.reshape(B, T, NSA_H, HD), W['nsa_qn']),
        'gates': jax.nn.sigmoid(ngt.reshape(B, T, NSA_H, 3)),
        'raw_kc': kv[:, :, 0], 'raw_vc': kv[:, :, 1],
        'k_sel': rms_norm(kv[:, :, 2], W['nsa_kn'][1]), 'v_sel': kv[:, :, 3],
        'k_win': rms_norm(kv[:, :, 4], W['nsa_kn'][2]), 'v_win': kv[:, :, 5],
        'q_nope': qm[..., :NOPE_D], 'q_rope': rope(qm[..., NOPE_D:], pos),
        'ckv': rms_norm(ckv, W['mla_ckvn']), 'kr': kr,
        'qmem': rms_norm(mq.reshape(B, T, MEM_H, HD), W['mem_qn']),
        'z': (nz, mz, memz),
    }


def even_finish(x, P, o_nsa, o_mla, o_mem, W):
    z_nsa, z_mla, z_mem = P['z']
    mixed = jnp.concatenate([o_nsa * jax.nn.silu(z_nsa), o_mla * jax.nn.silu(z_mla), o_mem * jax.nn.silu(z_mem)], axis=-1)
    B, T, D = x.shape
    return _matmul(mixed.reshape(B * T, -1), W['w_out'], residual=x.reshape(B * T, D)).reshape(B, T, D)


def _nsa_prompt(P, kc, vc, bias_t):
    q = P['q']
    B, T = q.shape[:2]
    nsb = T // SEL_BLK
    t = min(ATT_TILE, T)
    o_c, sel = _nsa_compressed_and_select(q, jnp.arange(T), kc, vc, nsb, bias_t)
    negmask = jnp.where(sel, 0.0, NEG).astype(jnp.bfloat16)
    negmask = jnp.transpose(negmask, (0, 2, 1, 3))[:, :, None]
    qq = _heads_first(q * HD ** -0.5, NSA_G).astype(jnp.bfloat16)
    o_s = _flash(qq, _heads_first(P['k_sel'], NSA_G).astype(jnp.bfloat16),
                 _heads_first(P['v_sel'], NSA_G)[:, :, 0].astype(jnp.bfloat16),
                 negmask, _expand_onehot(T, SEL_BLK, nsb, t),
                 _bias_tiles(bias_t, NSA_HPG, t, min(T5_NEAR_TILES, T // t)), mode='causal', name="nsa_selected_prompt")
    n_win = -(-(NSA_WINDOW - 1) // t) + 1
    o_w = _flash(qq, _heads_first(P['k_win'], NSA_G).astype(jnp.bfloat16),
                 _heads_first(P['v_win'], NSA_G)[:, :, 0].astype(jnp.bfloat16),
                 bias=_bias_tiles(bias_t, NSA_HPG, t, min(n_win, T // t), window=NSA_WINDOW),
                 mode='window', name="nsa_window_prompt")
    g = P['gates'].reshape(B, T, NSA_G, NSA_HPG, 3, 1)
    o_s = _heads_last(o_s).reshape(B, T, NSA_G, NSA_HPG, HD)
    o_w = _heads_last(o_w).reshape(B, T, NSA_G, NSA_HPG, HD)
    o = g[..., 0, :] * o_c + g[..., 1, :] * o_s + g[..., 2, :] * o_w
    return o.reshape(B, T, NSA_H * HD)


def even_prompt(x, ln, W, km, vm, bias_t):
    B, T = x.shape[:2]
    pos = jnp.arange(T, dtype=jnp.int32)
    P = even_project(x, ln, W, pos)
    kc, vc = nsa_compress([P['raw_kc']], [P['raw_vc']], W)
    o_nsa = _nsa_prompt(P, kc, vc, bias_t)
    o_mla = _mla_prompt(P, pos, W)
    o_mem = _mem_attend_prompt(P['qmem'], km, vm)
    y = even_finish(x, P, o_nsa, o_mla, o_mem, W)
    wk = min(NSA_WINDOW, T)
    state = (P['raw_kc'], P['raw_vc'], P['k_sel'], P['v_sel'], P['k_win'][:, T - wk:], P['v_win'][:, T - wk:], P['ckv'], P['kr'])
    return y, state


def even_sample(x, ln, W, km, vm, bias_t, caches, li, page_table):
    c_cmp_k, c_cmp_v, c_sel_k, c_sel_v, s_win_k, s_win_v, c_ckv, c_kr = caches
    B, S = x.shape[:2]
    n_pages = page_table.shape[1]
    past = n_pages * PAGE_SIZE
    total = past + S
    pos = past + jnp.arange(S, dtype=jnp.int32)
    pt_flat = page_table.reshape(-1)
    P = even_project(x, ln, W, pos)
    l_pad = -(-total // SEL_BLK) * SEL_BLK
    padn = ((0, 0), (0, l_pad - total), (0, 0), (0, 0))
    kc, vc = _nsa_compress_sample((c_cmp_k, c_cmp_v), li, pt_flat, B, n_pages,
                                  (jnp.pad(P['raw_kc'], padn), jnp.pad(P['raw_vc'], padn)), W)
    nsb = l_pad // SEL_BLK
    o_c, sel = _nsa_compressed_and_select(P['q'], pos, kc, vc, nsb, bias_t)
    nbp = past // SEL_BLK
    sel = sel[..., :nbp]
    sel_rows = jnp.broadcast_to(jnp.transpose(sel, (0, 2, 1, 3))[:, :, None], (B, NSA_G, NSA_HPG, S, nbp))
    o_s = _gqa_sample_attention(P['q'], sel_rows.reshape(B, NSA_H * S, nbp), c_sel_k, c_sel_v, li, pt_flat, n_pages,
                                SEL_BLK, P['k_sel'], P['v_sel'], bias_t, NSA_G, "nsa_selected_sample")
    kw = jnp.concatenate([s_win_k[li], P['k_win']], axis=1)
    vw = jnp.concatenate([s_win_v[li], P['v_win']], axis=1)
    wb = s_win_k.shape[2]
    kw_pos = past - wb + jnp.arange(wb + S)
    o_w = _nsa_window_sample(P['q'], pos, kw, vw, kw_pos, bias_t)
    g = P['gates'].reshape(B, S, NSA_G, NSA_HPG, 3, 1)
    o_nsa = (g[..., 0, :] * o_c + g[..., 1, :] * o_s.reshape(B, S, NSA_G, NSA_HPG, HD) + g[..., 2, :] * o_w)
    o_nsa = o_nsa.reshape(B, S, NSA_H * HD)
    o_mla = _mla_sample(P, c_ckv, c_kr, li, pt_flat, n_pages, W)
    o_mem = mem_attend(P['qmem'], km, vm)
    y = even_finish(x, P, o_nsa, o_mla, o_mem, W)
    wk = min(NSA_WINDOW, total)
    n_w = kw.shape[1]
    state = (P['raw_kc'], P['raw_vc'], P['k_sel'], P['v_sel'], kw[:, n_w - wk:], vw[:, n_w - wk:], P['ckv'], P['kr'])
    return y, state


def odd_project(x, ln, W):
    B, T = x.shape[:2]
    pieces = _project_split(x.reshape(B * T, -1), W['w_in'], ODD_SIZES, ln)
    mq, mkv, mz, memq, memz = [p.reshape(B, T, -1) for p in pieces]
    kv = mkv.reshape(B, T, 2, MOBA_KVH, HD)
    return {
        'q': rms_norm(mq.reshape(B, T, MOBA_H, HD), W['moba_qn']),
        'k': rms_norm(kv[:, :, 0], W['moba_kn']), 'v': kv[:, :, 1],
        'qmem': rms_norm(memq.reshape(B, T, MEM_H, HD), W['mem_qn']),
        'z': (mz, memz),
    }


def odd_finish(x, P, o_moba, o_mem, W):
    z_moba, z_mem = P['z']
    mixed = jnp.concatenate([o_moba * jax.nn.silu(z_moba), o_mem * jax.nn.silu(z_mem)], axis=-1)
    B, T, D = x.shape
    return _matmul(mixed.reshape(B * T, -1), W['w_out'], residual=x.reshape(B * T, D)).reshape(B, T, D)


def _moba_prompt(P, bias_t):
    q, k, v = P['q'], P['k'], P['v']
    B, T = q.shape[:2]
    assert T % MOBA_BLK == 0 and ATT_TILE == MOBA_BLK
    nbm = T // MOBA_BLK
    t = min(ATT_TILE, T)
    own = jnp.arange(T) // MOBA_BLK
    blk = jnp.arange(nbm)
    kmean = k.astype(jnp.float32).reshape(B, nbm, MOBA_BLK, MOBA_KVH, HD).sum(axis=2) / MOBA_BLK
    kmean_h = jnp.repeat(kmean, MOBA_HPG, axis=2)
    sel = (blk[None, :] == own[:, None])[None, :, None, :] | _moba_pick_blocks(q, jnp.arange(T), kmean_h)
    negmask = jnp.where(sel, 0.0, NEG).astype(jnp.bfloat16)
    negmask = jnp.transpose(negmask.reshape(B, T, MOBA_KVH, MOBA_HPG, nbm), (0, 2, 3, 1, 4))
    qq = _heads_first(q * HD ** -0.5, MOBA_KVH).astype(jnp.bfloat16)
    o = _flash(qq, _heads_first(k, MOBA_KVH).astype(jnp.bfloat16), _heads_first(v, MOBA_KVH)[:, :, 0].astype(jnp.bfloat16),
               negmask, _expand_onehot(T, MOBA_BLK, nbm, t),
               _bias_tiles(bias_t, MOBA_HPG, t, min(T5_NEAR_TILES, T // t)), mode='causal', name="moba_prompt")
    return _heads_last(o).reshape(B, T, MOBA_H * HD)


def odd_prompt(x, ln, W, km, vm, bias_t):
    P = odd_project(x, ln, W)
    o_moba = _moba_prompt(P, bias_t)
    y = odd_finish(x, P, o_moba, _mem_attend_prompt(P['qmem'], km, vm), W)
    return y, (P['k'], P['v'])


def odd_sample(x, ln, W, km, vm, bias_t, c_k, c_v, li, page_table):
    B, S = x.shape[:2]
    n_pages = page_table.shape[1]
    past = n_pages * PAGE_SIZE
    pt_flat = page_table.reshape(-1)
    P = odd_project(x, ln, W)
    page_sum = _paged_sum(_transposed_pages(c_k), li, pt_flat, B, n_pages)
    kmean = moba_means_paged(page_sum.reshape(B, n_pages, MOBA_KVH, HD), n_pages, P['k'])
    kmean_h = jnp.repeat(kmean, MOBA_HPG, axis=2)
    nbp = past // MOBA_BLK
    picked = _moba_pick_blocks(P['q'], past + jnp.arange(S), kmean_h)[..., :nbp]
    sel_rows = jnp.transpose(picked, (0, 2, 1, 3)).reshape(B, MOBA_H * S, nbp)
    o_moba = _gqa_sample_attention(P['q'], sel_rows, c_k, c_v, li, pt_flat, n_pages, MOBA_BLK, P['k'], P['v'], bias_t,
                                   MOBA_KVH, "moba_sample").reshape(B, S, MOBA_H * HD)
    y = odd_finish(x, P, o_moba, mem_attend(P['qmem'], km, vm), W)
    return y, (P['k'], P['v'])


def kernel(x_prompt, x_sample, cache_nsa_cmp_k, cache_nsa_cmp_v, cache_nsa_sel_k, cache_nsa_sel_v, state_nsa_win_k, state_nsa_win_v, cache_mla_ckv, cache_mla_krope, cache_moba_k, cache_moba_v, cache_mem_k, cache_mem_v, page_table, mem_prompt, rel_bias, ln_g, mem_norm_g, mem_wkv, mem_qn, mem_kn, e_w_in, e_w_out, nsa_qn, nsa_kn, nsa_cmp_pe, nsa_cmp_w1, nsa_cmp_w2, mla_cqn, mla_wuq, mla_ckvn, mla_wuk, mla_wuv, mla_qn, mla_kn, o_w_in, o_w_out, moba_qn, moba_kn):
    xp, xs = x_prompt, x_sample
    depth = ln_g.shape[0]
    even_caches = (cache_nsa_cmp_k, cache_nsa_cmp_v, cache_nsa_sel_k, cache_nsa_sel_v,
                   state_nsa_win_k, state_nsa_win_v, cache_mla_ckv, cache_mla_krope)
    even_p, even_s, odd_p, odd_s, mem_k_p, mem_v_p = [], [], [], [], [], []
    for i in range(depth):
        km_p, vm_p = mem_kv(mem_prompt, mem_norm_g[i], mem_wkv[i], mem_kn[i])
        mem_k_p.append(km_p)
        mem_v_p.append(vm_p)
        li = i // 2
        if i % 2 == 0:
            W = {'w_in': e_w_in[li], 'w_out': e_w_out[li], 'nsa_qn': nsa_qn[li], 'nsa_kn': nsa_kn[li],
                 'cmp_pe': nsa_cmp_pe[li], 'cmp_w1': nsa_cmp_w1[li], 'cmp_w2': nsa_cmp_w2[li],
                 'mla_cqn': mla_cqn[li], 'mla_wuq': mla_wuq[li], 'mla_ckvn': mla_ckvn[li],
                 'mla_wuk': mla_wuk[li], 'mla_wuv': mla_wuv[li], 'mla_qn': mla_qn[li], 'mla_kn': mla_kn[li],
                 'mem_qn': mem_qn[i]}
            xp, st_p = even_prompt(xp, ln_g[i], W, km_p, vm_p, rel_bias)
            xs, st_s = even_sample(xs, ln_g[i], W, cache_mem_k[i], cache_mem_v[i], rel_bias, even_caches, li, page_table)
            even_p.append(st_p)
            even_s.append(st_s)
        else:
            W = {'w_in': o_w_in[li], 'w_out': o_w_out[li], 'moba_qn': moba_qn[li], 'moba_kn': moba_kn[li],
                 'mem_qn': mem_qn[i]}
            xp, st_p = odd_prompt(xp, ln_g[i], W, km_p, vm_p, rel_bias)
            xs, st_s = odd_sample(xs, ln_g[i], W, cache_mem_k[i], cache_mem_v[i], rel_bias,
                                  cache_moba_k, cache_moba_v, li, page_table)
            odd_p.append(st_p)
            odd_s.append(st_s)

    def stk(lst, j):
        return jnp.stack([t[j] for t in lst])

    return (xp, xs,
            stk(even_p, 0), stk(even_p, 1), stk(even_p, 2), stk(even_p, 3),
            stk(even_p, 4), stk(even_p, 5), stk(even_p, 6), stk(even_p, 7),
            stk(odd_p, 0), stk(odd_p, 1), jnp.stack(mem_k_p), jnp.stack(mem_v_p),
            stk(even_s, 0), stk(even_s, 1), stk(even_s, 2), stk(even_s, 3),
            stk(even_s, 4), stk(even_s, 5), stk(even_s, 6), stk(even_s, 7),
            stk(odd_s, 0), stk(odd_s, 1))
```
